```python
import jax, jax.numpy as jnp
from jax import lax
import numpy as np

D_MODEL = 1024
BATCH = 8
SEQ = 8192
DEPTH = 1

D_MIX = D_MODEL
W_CONV = D_MIX // 2
W_LRU = D_MIX - W_CONV
N_CONV_GROUPS = 8
N_LRU_HEADS = 8
LRU_HEAD_DIM = W_LRU // N_LRU_HEADS
CONV_WIDTH = 31
LRU_CONV_WIDTH = 4
LRU_C = 8.0
D_FF = 2816
FFN_RES_SCALE = 0.5
RMS_EPS = 1e-6
LN_EPS = 1e-5

kernel_name = "macaron_conformer_conv_rglru_hybrid"


def rmsnorm(x, g):
    xf = x.astype(jnp.float32)
    y = xf * lax.rsqrt(jnp.mean(xf * xf, axis=-1, keepdims=True) + RMS_EPS)
    return (y * g.astype(jnp.float32)).astype(x.dtype)


def layernorm(x, g, b):
    xf = x.astype(jnp.float32)
    mu = jnp.mean(xf, axis=-1, keepdims=True)
    xc = xf - mu
    var = jnp.mean(xc * xc, axis=-1, keepdims=True)
    y = xc * lax.rsqrt(var + LN_EPS)
    return (y * g.astype(jnp.float32) + b.astype(jnp.float32)).astype(x.dtype)


def swiglu_ffn(x, w_gate, w_up, w_down):
    return (jax.nn.silu(x @ w_gate) * (x @ w_up)) @ w_down


def causal_depthwise_conv(x, w, b):
    k = w.shape[0]
    c = x.shape[-1]
    out = lax.conv_general_dilated(
        x, w[:, None, :].astype(x.dtype), window_strides=(1,), padding=[(k - 1, 0)],
        dimension_numbers=("NWC", "WIO", "NWC"), feature_group_count=c)
    return out + b.astype(x.dtype)


def block_diag_linear(x, w, b):
    bsz, s, _ = x.shape
    xh = x.reshape(bsz, s, N_LRU_HEADS, LRU_HEAD_DIM)
    y = jnp.einsum("bshi,hij->bshj", xh, w.astype(x.dtype))
    return y.reshape(bsz, s, W_LRU) + b.astype(x.dtype)


def rg_lru(x, w_a, b_a, w_x, b_x, lam):
    xf = x.astype(jnp.float32)
    r = jax.nn.sigmoid(block_diag_linear(x, w_a, b_a).astype(jnp.float32))
    i = jax.nn.sigmoid(block_diag_linear(x, w_x, b_x).astype(jnp.float32))
    log_a = -LRU_C * r * jax.nn.softplus(-lam.astype(jnp.float32))
    a = jnp.exp(log_a)
    mult = jnp.sqrt(-jnp.expm1(2.0 * log_a))
    bterm = mult * (i * xf)

    def combine(lhs, rhs):
        a1, b1 = lhs
        a2, b2 = rhs
        return a1 * a2, a2 * b1 + b2

    _, h = lax.associative_scan(combine, (a, bterm), axis=1)
    return h.astype(x.dtype)


def hybrid_mixer(h, w_in, conv_dw, conv_dw_bias, conv_ln_g, conv_ln_b,
                 lru_conv_w, lru_conv_b, lru_w_a, lru_b_a, lru_w_x, lru_b_x, lru_lambda, w_out):
    z = h @ w_in
    c_val, c_gate, r_x, r_gate = jnp.split(
        z, [W_CONV, 2 * W_CONV, 2 * W_CONV + W_LRU], axis=-1)
    u = c_val * jax.nn.sigmoid(c_gate)
    u = causal_depthwise_conv(u, conv_dw, conv_dw_bias)
    u = jax.nn.silu(layernorm(u, conv_ln_g, conv_ln_b))
    xr = causal_depthwise_conv(r_x, lru_conv_w, lru_conv_b)
    yr = rg_lru(xr, lru_w_a, lru_b_a, lru_w_x, lru_b_x, lru_lambda)
    yr = yr * jax.nn.gelu(r_gate, approximate=True)
    return jnp.concatenate([u, yr], axis=-1) @ w_out


def setup_inputs(seed: int = 0) -> dict:
    key = jax.random.key(seed)
    ks = jax.random.split(key, 32)
    f32 = jnp.float32
    nrm = lambda k, shape, scale: jax.random.normal(k, shape, f32) * scale
    gain = lambda k, n: 1.0 + 0.02 * jax.random.normal(k, (n,), f32)
    d_in = 2 * W_CONV + 2 * W_LRU
    u = jax.random.uniform(ks[20], (W_LRU,), f32, 0.9, 0.999)
    a0 = u ** (1.0 / LRU_C)
    lru_lambda = jnp.log(a0) - jnp.log1p(-a0)
    return {
        "x": jax.random.normal(ks[0], (BATCH, SEQ, D_MODEL), f32),
        "ffn1_norm": gain(ks[1], D_MODEL),
        "ffn1_w_gate": nrm(ks[2], (D_MODEL, D_FF), D_MODEL ** -0.5),
        "ffn1_w_up": nrm(ks[3], (D_MODEL, D_FF), D_MODEL ** -0.5),
        "ffn1_w_down": nrm(ks[4], (D_FF, D_MODEL), D_FF ** -0.5),
        "mix_norm": gain(ks[5], D_MODEL),
        "w_in": nrm(ks[6], (D_MODEL, d_in), D_MODEL ** -0.5),
        "conv_dw": nrm(ks[7], (CONV_WIDTH, W_CONV), CONV_WIDTH ** -0.5),
        "conv_dw_bias": nrm(ks[8], (W_CONV,), 0.02),
        "conv_ln_g": gain(ks[9], W_CONV),
        "conv_ln_b": nrm(ks[10], (W_CONV,), 0.02),
        "lru_conv_w": nrm(ks[11], (LRU_CONV_WIDTH, W_LRU), LRU_CONV_WIDTH ** -0.5),
        "lru_conv_b": nrm(ks[12], (W_LRU,), 0.02),
        "lru_w_a": nrm(ks[13], (N_LRU_HEADS, LRU_HEAD_DIM, LRU_HEAD_DIM), LRU_HEAD_DIM ** -0.5),
        "lru_b_a": nrm(ks[14], (W_LRU,), 0.02),
        "lru_w_x": nrm(ks[15], (N_LRU_HEADS, LRU_HEAD_DIM, LRU_HEAD_DIM), LRU_HEAD_DIM ** -0.5),
        "lru_b_x": nrm(ks[16], (W_LRU,), 0.02),
        "lru_lambda": lru_lambda,
        "w_out": nrm(ks[17], (D_MIX, D_MODEL), D_MIX ** -0.5),
        "ffn2_norm": gain(ks[18], D_MODEL),
        "ffn2_w_gate": nrm(ks[19], (D_MODEL, D_FF), D_MODEL ** -0.5),
        "ffn2_w_up": nrm(ks[21], (D_MODEL, D_FF), D_MODEL ** -0.5),
        "ffn2_w_down": nrm(ks[22], (D_FF, D_MODEL), D_FF ** -0.5),
        "final_norm": gain(ks[23], D_MODEL),
    }


def reference(x, ffn1_norm, ffn1_w_gate, ffn1_w_up, ffn1_w_down, mix_norm, w_in,
              conv_dw, conv_dw_bias, conv_ln_g, conv_ln_b, lru_conv_w, lru_conv_b,
              lru_w_a, lru_b_a, lru_w_x, lru_b_x, lru_lambda, w_out,
              ffn2_norm, ffn2_w_gate, ffn2_w_up, ffn2_w_down, final_norm):
    for _ in range(DEPTH):
        x = x + FFN_RES_SCALE * swiglu_ffn(rmsnorm(x, ffn1_norm), ffn1_w_gate, ffn1_w_up, ffn1_w_down)
        x = x + hybrid_mixer(rmsnorm(x, mix_norm), w_in, conv_dw, conv_dw_bias, conv_ln_g, conv_ln_b,
                             lru_conv_w, lru_conv_b, lru_w_a, lru_b_a, lru_w_x, lru_b_x,
                             lru_lambda, w_out)
        x = x + FFN_RES_SCALE * swiglu_ffn(rmsnorm(x, ffn2_norm), ffn2_w_gate, ffn2_w_up, ffn2_w_down)
    return rmsnorm(x, final_norm)
```

```python
import functools

import jax
import jax.numpy as jnp
from jax import lax
from jax.experimental import pallas as pl
from jax.experimental.pallas import tpu as pltpu
from jax.scipy.linalg import block_diag

FFN_RES_SCALE = 0.5
RMS_EPS = 1e-6
LN_EPS = 1e-5
LRU_C = 8.0

SUBLANES = 8
CONV_HALO = 32
LRU_HALO = 8
CONV_ROWS = 32

FFN_TILE = 512
MIX_TILE = 512
VMEM_LIMIT = 56 * 1024 * 1024


def _rmsnorm(x, g):
    ms = jnp.mean(x * x, axis=-1, keepdims=True)
    return x * lax.rsqrt(ms + RMS_EPS) * g


def _sigmoid(x):
    return 1.0 / (1.0 + jnp.exp(-x))


def _const_spec(shape):
    nd = len(shape)
    return pl.BlockSpec(shape, lambda *_: (0,) * nd, pipeline_mode=pl.Buffered(1))


def _ffn_kernel(x_ref, g_ref, wg_ref, wu_ref, wd_ref, fg_ref, o_ref, *, final_norm):
    x = x_ref[...]
    xn = _rmsnorm(x, g_ref[...]).astype(jnp.bfloat16)
    gate = jnp.dot(xn, wg_ref[...], preferred_element_type=jnp.float32)
    up = jnp.dot(xn, wu_ref[...], preferred_element_type=jnp.float32)
    act = (gate * _sigmoid(gate) * up).astype(jnp.bfloat16)
    y = jnp.dot(act, wd_ref[...], preferred_element_type=jnp.float32)
    out = x + FFN_RES_SCALE * y
    if final_norm:
        out = _rmsnorm(out, fg_ref[...])
    o_ref[...] = out


def _ffn(x2d, norm_g, w_gate, w_up, w_down, final_g, *, final_norm, name):
    n, d = x2d.shape
    d_ff = w_gate.shape[1]
    tm = FFN_TILE
    return pl.pallas_call(
        functools.partial(_ffn_kernel, final_norm=final_norm),
        grid=(n // tm,),
        in_specs=[
            pl.BlockSpec((tm, d), lambda i: (i, 0)),
            _const_spec((1, d)),
            _const_spec((d, d_ff)),
            _const_spec((d, d_ff)),
            _const_spec((d_ff, d)),
            _const_spec((1, d)),
        ],
        out_specs=pl.BlockSpec((tm, d), lambda i: (i, 0)),
        out_shape=jax.ShapeDtypeStruct((n, d), jnp.float32),
        compiler_params=pltpu.CompilerParams(
            dimension_semantics=("arbitrary",), vmem_limit_bytes=VMEM_LIMIT),
        name=name,
    )(x2d, norm_g, w_gate, w_up, w_down, final_g)


def _mixer_kernel(x_ref, g_ref, win_ref, cw_ref, cb_ref, lng_ref, lnb_ref,
                  rw_ref, rb_ref, wgate_ref, bgate_ref, lam_ref, wout_ref,
                  o_ref,
                  ubuf, rbuf, cbuf, abuf, bbuf, hbuf, hcarry):
    ts = x_ref.shape[1]
    wc = cw_ref.shape[1]
    wl = rw_ref.shape[1]
    kc = cw_ref.shape[0]
    kr = rw_ref.shape[0]
    j = pl.program_id(1)

    @pl.when(j == 0)
    def _():
        ubuf[0:CONV_HALO, :] = jnp.zeros((CONV_HALO, wc), jnp.float32)
        rbuf[0:LRU_HALO, :] = jnp.zeros((LRU_HALO, wl), jnp.float32)
        hcarry[...] = jnp.zeros_like(hcarry)

    x = x_ref[0]
    hn = _rmsnorm(x, g_ref[...]).astype(jnp.bfloat16)
    z = jnp.dot(hn, win_ref[...], preferred_element_type=jnp.float32)
    c_val = z[:, 0:wc]
    c_gate = z[:, wc:2 * wc]
    r_x = z[:, 2 * wc:2 * wc + wl]
    r_gate = z[:, 2 * wc + wl:]

    ubuf[CONV_HALO:CONV_HALO + ts, :] = c_val * _sigmoid(c_gate)
    rbuf[LRU_HALO:LRU_HALO + ts, :] = r_x

    cbias = cb_ref[...]

    for c in range(ts // CONV_ROWS):
        r0 = c * CONV_ROWS
        acc = jnp.broadcast_to(cbias, (CONV_ROWS, wc))
        for k in range(kc):
            off = r0 + CONV_HALO - (kc - 1) + k
            acc = acc + cw_ref[k:k + 1, :] * ubuf[off:off + CONV_ROWS, :]
        cbuf[r0:r0 + CONV_ROWS, :] = acc
    ubuf[0:CONV_HALO, :] = ubuf[ts:ts + CONV_HALO, :]

    cv = cbuf[...]
    mu = jnp.mean(cv, axis=-1, keepdims=True)
    cc = cv - mu
    var = jnp.mean(cc * cc, axis=-1, keepdims=True)
    ln = cc * lax.rsqrt(var + LN_EPS) * lng_ref[...] + lnb_ref[...]
    ua = ln * _sigmoid(ln)

    xr = jnp.broadcast_to(rb_ref[...], (ts, wl))
    for k in range(kr):
        off = LRU_HALO - (kr - 1) + k
        xr = xr + rw_ref[k:k + 1, :] * rbuf[off:off + ts, :]
    rbuf[0:LRU_HALO, :] = rbuf[ts:ts + LRU_HALO, :]

    gates = jnp.dot(xr.astype(jnp.bfloat16), wgate_ref[...],
                    preferred_element_type=jnp.float32) + bgate_ref[...]
    r = _sigmoid(gates[:, 0:wl])
    i = _sigmoid(gates[:, wl:])
    lam = lam_ref[...]
    softplus_neg_lam = jnp.maximum(-lam, 0.0) + jnp.log1p(jnp.exp(-jnp.abs(lam)))
    log_a = (-LRU_C) * r * softplus_neg_lam
    a = jnp.exp(log_a)
    mult = jnp.sqrt(jnp.tanh(-log_a) * (1.0 + a * a))
    b = mult * (i * xr)

    g3 = ts // SUBLANES
    a3 = a.reshape(g3, SUBLANES, wl)
    b3 = b.reshape(g3, SUBLANES, wl)
    sub = lax.broadcasted_iota(jnp.int32, (g3, SUBLANES, wl), 1)
    d = 1
    while d < SUBLANES:
        keep = sub >= d
        a_prev = jnp.where(keep, pltpu.roll(a3, d, axis=1), 1.0)
        b_prev = jnp.where(keep, pltpu.roll(b3, d, axis=1), 0.0)
        b3 = a3 * b_prev + b3
        a3 = a3 * a_prev
        d *= 2
    abuf[...] = a3.reshape(ts, wl)
    bbuf[...] = b3.reshape(ts, wl)

    def scan_rows(g, h_prev):
        r0 = pl.multiple_of(g * SUBLANES, SUBLANES)
        h = abuf[pl.ds(r0, SUBLANES), :] * h_prev + bbuf[pl.ds(r0, SUBLANES), :]
        hbuf[pl.ds(r0, SUBLANES), :] = h
        return jnp.broadcast_to(h[SUBLANES - 1:SUBLANES, :], (SUBLANES, wl))

    h_last = lax.fori_loop(0, g3, scan_rows, hcarry[...], unroll=8)
    hcarry[...] = h_last

    gelu = 0.5 * r_gate * (1.0 + jnp.tanh(
        0.7978845608028654 * (r_gate + 0.044715 * (r_gate * r_gate * r_gate))))
    yr = hbuf[...] * gelu

    y = jnp.dot(ua.astype(jnp.bfloat16), wout_ref[0:wc, :],
                preferred_element_type=jnp.float32)
    y = y + jnp.dot(yr.astype(jnp.bfloat16), wout_ref[wc:, :],
                    preferred_element_type=jnp.float32)
    o_ref[0] = x + y


def _mixer(x, mix_norm, w_in, conv_dw, conv_dw_bias, conv_ln_g, conv_ln_b,
           lru_conv_w, lru_conv_b, w_gates, b_gates, lru_lambda, w_out):
    bsz, s, d = x.shape
    ts = MIX_TILE
    wc = conv_dw.shape[1]
    wl = lru_conv_w.shape[1]
    f32 = jnp.float32
    consts = (mix_norm, w_in, conv_dw, conv_dw_bias, conv_ln_g, conv_ln_b,
              lru_conv_w, lru_conv_b, w_gates, b_gates, lru_lambda, w_out)
    return pl.pallas_call(
        _mixer_kernel,
        grid=(bsz, s // ts),
        in_specs=[pl.BlockSpec((1, ts, d), lambda b, j: (b, j, 0))]
        + [_const_spec(c.shape) for c in consts],
        out_specs=pl.BlockSpec((1, ts, d), lambda b, j: (b, j, 0)),
        out_shape=jax.ShapeDtypeStruct((bsz, s, d), f32),
        scratch_shapes=[
            pltpu.VMEM((CONV_HALO + ts, wc), f32),
            pltpu.VMEM((LRU_HALO + ts, wl), f32),
            pltpu.VMEM((ts, wc), f32),
            pltpu.VMEM((ts, wl), f32),
            pltpu.VMEM((ts, wl), f32),
            pltpu.VMEM((ts, wl), f32),
            pltpu.VMEM((SUBLANES, wl), f32),
        ],
        compiler_params=pltpu.CompilerParams(
            dimension_semantics=("arbitrary", "arbitrary"),
            vmem_limit_bytes=VMEM_LIMIT),
        name="hybrid_mixer",
    )(x, *consts)


def kernel(x, ffn1_norm, ffn1_w_gate, ffn1_w_up, ffn1_w_down, mix_norm, w_in,
           conv_dw, conv_dw_bias, conv_ln_g, conv_ln_b, lru_conv_w, lru_conv_b,
           lru_w_a, lru_b_a, lru_w_x, lru_b_x, lru_lambda, w_out,
           ffn2_norm, ffn2_w_gate, ffn2_w_up, ffn2_w_down, final_norm):
    bsz, s, d = x.shape
    bf16 = jnp.bfloat16
    row = lambda v: v.reshape(1, -1)

    w_gates = jnp.concatenate(
        [block_diag(*lru_w_a), block_diag(*lru_w_x)], axis=1).astype(bf16)
    b_gates = jnp.concatenate([lru_b_a, lru_b_x]).reshape(1, -1)

    x1 = _ffn(x.reshape(bsz * s, d), row(ffn1_norm), ffn1_w_gate.astype(bf16),
              ffn1_w_up.astype(bf16), ffn1_w_down.astype(bf16), row(final_norm),
              final_norm=False, name="ffn1")
    x2 = _mixer(x1.reshape(bsz, s, d), row(mix_norm), w_in.astype(bf16),
                conv_dw, row(conv_dw_bias), row(conv_ln_g), row(conv_ln_b),
                lru_conv_w, row(lru_conv_b), w_gates, b_gates, row(lru_lambda),
                w_out.astype(bf16))
    y = _ffn(x2.reshape(bsz * s, d), row(ffn2_norm), ffn2_w_gate.astype(bf16),
             ffn2_w_up.astype(bf16), ffn2_w_down.astype(bf16), row(final_norm),
             final_norm=True, name="ffn2_final")
    return y.reshape(bsz, s, d)
```

```python
import functools

import jax
import jax.numpy as jnp
from jax import lax
from jax.experimental import pallas as pl
from jax.experimental.pallas import tpu as pltpu
from jax.scipy.linalg import block_diag

FFN_RES_SCALE = 0.5
RMS_EPS = 1e-6
LN_EPS = 1e-5
LRU_C = 8.0

SUBLANES = 8
LANES = 128
CONV_HALO = 32
LRU_HALO = 8

FFN_TILE = 512
MIX_T = 64
VMEM_LIMIT = 56 * 1024 * 1024

NEG_LOG2_E = -1.4426950408889634
GELU_C0 = 0.7978845608028654
GELU_C1 = 0.7978845608028654 * 0.044715


def _rmsnorm(x, g):
    ms = jnp.mean(x * x, axis=-1, keepdims=True)
    return x * lax.rsqrt(ms + RMS_EPS) * g


def _sigmoid(x):
    return 1.0 / (1.0 + jnp.exp2(x * NEG_LOG2_E))


def _const_spec(shape):
    nd = len(shape)
    return pl.BlockSpec(shape, lambda *_: (0,) * nd, pipeline_mode=pl.Buffered(1))


def _ffn_kernel(x_ref, g_ref, wg_ref, wu_ref, wd_ref, fg_ref, o_ref, *, final_norm):
    x = x_ref[...]
    xn = _rmsnorm(x, g_ref[...]).astype(jnp.bfloat16)
    gate = jnp.dot(xn, wg_ref[...], preferred_element_type=jnp.float32)
    up = jnp.dot(xn, wu_ref[...], preferred_element_type=jnp.float32)
    act = (gate * _sigmoid(gate) * up).astype(jnp.bfloat16)
    y = jnp.dot(act, wd_ref[...], preferred_element_type=jnp.float32)
    out = x + FFN_RES_SCALE * y
    if final_norm:
        out = _rmsnorm(out, fg_ref[...])
    o_ref[...] = out


def _ffn(x2d, norm_g, w_gate, w_up, w_down, final_g, *, final_norm, name):
    n, d = x2d.shape
    d_ff = w_gate.shape[1]
    tm = FFN_TILE
    return pl.pallas_call(
        functools.partial(_ffn_kernel, final_norm=final_norm),
        grid=(n // tm,),
        in_specs=[
            pl.BlockSpec((tm, d), lambda i: (i, 0)),
            _const_spec((1, d)),
            _const_spec((d, d_ff)),
            _const_spec((d, d_ff)),
            _const_spec((d_ff, d)),
            _const_spec((1, d)),
        ],
        out_specs=pl.BlockSpec((tm, d), lambda i: (i, 0)),
        out_shape=jax.ShapeDtypeStruct((n, d), jnp.float32),
        compiler_params=pltpu.CompilerParams(
            dimension_semantics=("arbitrary",), vmem_limit_bytes=VMEM_LIMIT),
        name=name,
    )(x2d, norm_g, w_gate, w_up, w_down, final_g)


def _lane_chunks(v):
    return [v[:, c * LANES:(c + 1) * LANES] for c in range(v.shape[1] // LANES)]


def _mixer_kernel(x_ref, g_ref, win_ref, cw_ref, cb_ref, lng_ref, lnb_ref,
                  rw_ref, rb_ref, wgate_ref, bgate_ref, lam_ref, wout_ref,
                  o_ref,
                  ubuf, cbuf, rbuf, xrbuf, abuf, bbuf, hbuf, hcarry):
    nb, t_len, d = x_ref.shape
    rows = nb * t_len
    wc = cw_ref.shape[1]
    wl = rw_ref.shape[1]
    kc = cw_ref.shape[0]
    kr = rw_ref.shape[0]
    ncc = wc // LANES
    ncl = wl // LANES
    pu = CONV_HALO + t_len
    pr = LRU_HALO + t_len
    half = t_len // 2
    f32 = jnp.float32

    @pl.when(pl.program_id(0) == 0)
    def _():
        for b in range(nb):
            ubuf[:, b * pu:b * pu + CONV_HALO, :] = jnp.zeros((ncc, CONV_HALO, LANES), f32)
            rbuf[:, b * pr:b * pr + LRU_HALO, :] = jnp.zeros((ncl, LRU_HALO, LANES), f32)
        hcarry[...] = jnp.zeros_like(hcarry)

    x = x_ref[...].reshape(rows, d)
    hn = _rmsnorm(x, g_ref[...]).astype(jnp.bfloat16)
    z = jnp.dot(hn, win_ref[...], preferred_element_type=f32)
    c_val = z[:, 0:wc]
    c_gate = z[:, wc:2 * wc]
    r_x = z[:, 2 * wc:2 * wc + wl]
    r_gate = z[:, 2 * wc + wl:]

    u = c_val * _sigmoid(c_gate)
    for c, uc in enumerate(_lane_chunks(u)):
        for b in range(nb):
            ubuf[c, b * pu + CONV_HALO:b * pu + pu, :] = uc[b * t_len:(b + 1) * t_len, :]
    for c, rc in enumerate(_lane_chunks(r_x)):
        for b in range(nb):
            rbuf[c, b * pr + LRU_HALO:b * pr + pr, :] = rc[b * t_len:(b + 1) * t_len, :]

    def conv_seq(b, carry):
        ubase = pl.multiple_of(b * pu, SUBLANES)
        obase = pl.multiple_of(b * t_len, SUBLANES)
        for c in range(ncc):
            lanes = slice(c * LANES, (c + 1) * LANES)
            acc = [jnp.broadcast_to(cb_ref[:, lanes], (half, LANES)) for _ in range(2)]
            for k in range(kc):
                wk = cw_ref[k:k + 1, lanes]
                for p in range(2):
                    off = CONV_HALO - (kc - 1) + k + p
                    acc[p] = acc[p] + wk * ubuf[c, pl.ds(ubase + off, half, stride=2), :]
            for p in range(2):
                cbuf[c, pl.ds(obase + p, half, stride=2), :] = acc[p]
        return carry

    lax.fori_loop(0, nb, conv_seq, 0)
    for b in range(nb):
        ubuf[:, b * pu:b * pu + CONV_HALO, :] = ubuf[:, b * pu + t_len:b * pu + pu, :]

    cv = jnp.concatenate([cbuf[c] for c in range(ncc)], axis=1)
    mu = jnp.mean(cv, axis=-1, keepdims=True)
    cc = cv - mu
    var = jnp.mean(cc * cc, axis=-1, keepdims=True)
    ln = cc * lax.rsqrt(var + LN_EPS) * lng_ref[...] + lnb_ref[...]
    ua = ln * _sigmoid(ln)

    for c in range(ncl):
        lanes = slice(c * LANES, (c + 1) * LANES)
        wk = [rw_ref[k:k + 1, lanes] for k in range(kr)]
        bias = rb_ref[:, lanes]
        window = [rbuf[c, pl.ds(LRU_HALO - (kr - 1) + i, nb, stride=pr), :]
                  for i in range(kr - 1)]
        for t in range(t_len):
            window.append(rbuf[c, pl.ds(LRU_HALO + t, nb, stride=pr), :])
            xr_t = bias + wk[0] * window[0]
            for k in range(1, kr):
                xr_t = xr_t + wk[k] * window[k]
            xrbuf[t * nb:(t + 1) * nb, lanes] = xr_t
            window = window[1:]
    for b in range(nb):
        rbuf[:, b * pr:b * pr + LRU_HALO, :] = rbuf[:, b * pr + t_len:b * pr + pr, :]

    xr = xrbuf[...]
    gates = jnp.dot(xr.astype(jnp.bfloat16), wgate_ref[...],
                    preferred_element_type=f32) + bgate_ref[...]
    r = _sigmoid(gates[:, 0:wl])
    i_gate = _sigmoid(gates[:, wl:])
    lam = lam_ref[...]
    softplus_neg_lam = jnp.maximum(-lam, 0.0) + jnp.log1p(jnp.exp(-jnp.abs(lam)))
    log_a = (-LRU_C) * r * softplus_neg_lam
    a = jnp.exp(log_a)
    mult = jnp.sqrt(jnp.tanh(-log_a) * (1.0 + a * a))
    abuf[...] = a
    bbuf[...] = mult * (i_gate * xr)

    h = hcarry[...]
    for t in range(t_len):
        h = abuf[t * nb:(t + 1) * nb, :] * h + bbuf[t * nb:(t + 1) * nb, :]
        for c, hc in enumerate(_lane_chunks(h)):
            hbuf[c, pl.ds(t, nb, stride=pr), :] = hc
    hcarry[...] = h

    hs = jnp.concatenate(
        [jnp.concatenate([hbuf[c, b * pr:b * pr + t_len, :] for c in range(ncl)], axis=1)
         for b in range(nb)], axis=0)
    half_g = 0.5 * r_gate
    tanh_g = jnp.tanh(r_gate * (GELU_C0 + GELU_C1 * (r_gate * r_gate)))
    yr = hs * (half_g + half_g * tanh_g)

    y = jnp.dot(ua.astype(jnp.bfloat16), wout_ref[0:wc, :], preferred_element_type=f32)
    y = y + jnp.dot(yr.astype(jnp.bfloat16), wout_ref[wc:, :], preferred_element_type=f32)
    o_ref[...] = (x + y).reshape(nb, t_len, d)


def _mixer(x, mix_norm, w_in, conv_dw, conv_dw_bias, conv_ln_g, conv_ln_b,
           lru_conv_w, lru_conv_b, w_gates, b_gates, lru_lambda, w_out):
    bsz, s, d = x.shape
    t_len = MIX_T
    rows = bsz * t_len
    wc = conv_dw.shape[1]
    wl = lru_conv_w.shape[1]
    assert bsz == SUBLANES and wc % LANES == 0 and wl % LANES == 0
    assert s % t_len == 0 and t_len % (2 * SUBLANES) == 0
    assert ((LRU_HALO + t_len) // SUBLANES) % 2 == 1
    f32 = jnp.float32
    consts = (mix_norm, w_in, conv_dw, conv_dw_bias, conv_ln_g, conv_ln_b,
              lru_conv_w, lru_conv_b, w_gates, b_gates, lru_lambda, w_out)
    return pl.pallas_call(
        _mixer_kernel,
        grid=(s // t_len,),
        in_specs=[pl.BlockSpec((bsz, t_len, d), lambda j: (0, j, 0))]
        + [_const_spec(c.shape) for c in consts],
        out_specs=pl.BlockSpec((bsz, t_len, d), lambda j: (0, j, 0)),
        out_shape=jax.ShapeDtypeStruct((bsz, s, d), f32),
        scratch_shapes=[
            pltpu.VMEM((wc // LANES, bsz * (CONV_HALO + t_len), LANES), f32),
            pltpu.VMEM((wc // LANES, rows, LANES), f32),
            pltpu.VMEM((wl // LANES, bsz * (LRU_HALO + t_len), LANES), f32),
            pltpu.VMEM((rows, wl), f32),
            pltpu.VMEM((rows, wl), f32),
            pltpu.VMEM((rows, wl), f32),
            pltpu.VMEM((wl // LANES, bsz * (LRU_HALO + t_len), LANES), f32),
            pltpu.VMEM((bsz, wl), f32),
        ],
        compiler_params=pltpu.CompilerParams(
            dimension_semantics=("arbitrary",), vmem_limit_bytes=VMEM_LIMIT),
        name="hybrid_mixer",
    )(x, *consts)


def kernel(x, ffn1_norm, ffn1_w_gate, ffn1_w_up, ffn1_w_down, mix_norm, w_in,
           conv_dw, conv_dw_bias, conv_ln_g, conv_ln_b, lru_conv_w, lru_conv_b,
           lru_w_a, lru_b_a, lru_w_x, lru_b_x, lru_lambda, w_out,
           ffn2_norm, ffn2_w_gate, ffn2_w_up, ffn2_w_down, final_norm):
    bsz, s, d = x.shape
    bf16 = jnp.bfloat16
    row = lambda v: v.reshape(1, -1)

    w_gates = jnp.concatenate(
        [block_diag(*lru_w_a), block_diag(*lru_w_x)], axis=1).astype(bf16)
    b_gates = jnp.concatenate([lru_b_a, lru_b_x]).reshape(1, -1)

    x1 = _ffn(x.reshape(bsz * s, d), row(ffn1_norm), ffn1_w_gate.astype(bf16),
              ffn1_w_up.astype(bf16), ffn1_w_down.astype(bf16), row(final_norm),
              final_norm=False, name="ffn1")
    x2 = _mixer(x1.reshape(bsz, s, d), row(mix_norm), w_in.astype(bf16),
                conv_dw, row(conv_dw_bias), row(conv_ln_g), row(conv_ln_b),
                lru_conv_w, row(lru_conv_b), w_gates, b_gates, row(lru_lambda),
                w_out.astype(bf16))
    y = _ffn(x2.reshape(bsz * s, d), row(ffn2_norm), ffn2_w_gate.astype(bf16),
             ffn2_w_up.astype(bf16), ffn2_w_down.astype(bf16), row(final_norm),
             final_norm=True, name="ffn2_final")
    return y.reshape(bsz, s, d)
```

```python
import functools

import jax
import jax.numpy as jnp
from jax import lax
from jax.experimental import pallas as pl
from jax.experimental.pallas import tpu as pltpu
from jax.scipy.linalg import block_diag

FFN_RES_SCALE = 0.5
RMS_EPS = 1e-6
LN_EPS = 1e-5
LRU_C = 8.0

SUBLANES = 8
LANES = 128
CONV_HALO = 32
LRU_HALO = 8

MXU_DEPTH = 256

FFN_TILE = 1024
FFN_CHUNKS = 2
MIX_T = 64
MIX_PROJ_CHUNKS = 4
VMEM_LIMIT = 58 * 1024 * 1024

NEG_LOG2_E = -1.4426950408889634
GELU_C0 = 0.7978845608028654
GELU_C1 = 0.7978845608028654 * 0.044715


def _rmsnorm(x, g):
    ms = jnp.mean(x * x, axis=-1, keepdims=True)
    return x * lax.rsqrt(ms + RMS_EPS) * g


def _sigmoid(x):
    return 1.0 / (1.0 + jnp.exp2(x * NEG_LOG2_E))


def _ffn_chunks(d_ff):
    tiles = -(-d_ff // MXU_DEPTH)
    per = -(-tiles // FFN_CHUNKS)
    edges = [min(q * per * MXU_DEPTH, d_ff) for q in range(FFN_CHUNKS + 1)]
    return [(lo, hi) for lo, hi in zip(edges[:-1], edges[1:]) if hi > lo]


def _const_spec(shape):
    nd = len(shape)
    return pl.BlockSpec(shape, lambda *_: (0,) * nd, pipeline_mode=pl.Buffered(1))


def _ffn_kernel(x_ref, g_ref, wg_ref, wu_ref, wd_ref, fg_ref, o_ref, *, final_norm):
    x = x_ref[...]
    xn = _rmsnorm(x, g_ref[...]).astype(jnp.bfloat16)
    d_ff = wg_ref.shape[1]
    y = None
    for lo, hi in _ffn_chunks(d_ff):
        gate = jnp.dot(xn, wg_ref[:, lo:hi], preferred_element_type=jnp.float32)
        up = jnp.dot(xn, wu_ref[:, lo:hi], preferred_element_type=jnp.float32)
        act = (gate * _sigmoid(gate) * up).astype(jnp.bfloat16)
        yq = jnp.dot(act, wd_ref[lo:hi, :], preferred_element_type=jnp.float32)
        y = yq if y is None else y + yq
    out = x + FFN_RES_SCALE * y
    if final_norm:
        out = _rmsnorm(out, fg_ref[...])
    o_ref[...] = out


def _ffn(x2d, norm_g, w_gate, w_up, w_down, final_g, *, final_norm, name):
    n, d = x2d.shape
    d_ff = w_gate.shape[1]
    tm = FFN_TILE
    return pl.pallas_call(
        functools.partial(_ffn_kernel, final_norm=final_norm),
        grid=(n // tm,),
        in_specs=[
            pl.BlockSpec((tm, d), lambda i: (i, 0)),
            _const_spec((1, d)),
            _const_spec((d, d_ff)),
            _const_spec((d, d_ff)),
            _const_spec((d_ff, d)),
            _const_spec((1, d)),
        ],
        out_specs=pl.BlockSpec((tm, d), lambda i: (i, 0)),
        out_shape=jax.ShapeDtypeStruct((n, d), jnp.float32),
        compiler_params=pltpu.CompilerParams(
            dimension_semantics=("arbitrary",), vmem_limit_bytes=VMEM_LIMIT),
        name=name,
    )(x2d, norm_g, w_gate, w_up, w_down, final_g)


def _lane_chunks(v):
    return [v[:, c * LANES:(c + 1) * LANES] for c in range(v.shape[1] // LANES)]


def _mixer_kernel(xn_ref, xp_ref, g_ref, win_ref, cw_ref, cb_ref, lng_ref, lnb_ref,
                  rw_ref, rb_ref, wgate_ref, bgate_ref, lam_ref, wout_ref,
                  o_ref,
                  zbuf, hnbuf, gbuf, ubuf, cbuf, rbuf, xrbuf, abuf, bbuf, hbuf, hcarry):
    nb, t_len, d = xn_ref.shape
    rows = nb * t_len
    wc = cw_ref.shape[1]
    wl = rw_ref.shape[1]
    kc = cw_ref.shape[0]
    kr = rw_ref.shape[0]
    ncc = wc // LANES
    ncl = wl // LANES
    pu = CONV_HALO + t_len
    pr = LRU_HALO + t_len
    half = t_len // 2
    f32 = jnp.float32

    j = pl.program_id(0)
    slot_new = j % 2
    slot_old = 1 - slot_new
    n_proj = win_ref.shape[0]
    wq = win_ref.shape[2]
    seq_per_iter = nb // n_proj

    def z_old(col0, rs, lanes):
        q, off = divmod(col0, wq)
        return zbuf[slot_old, q, rs, off + lanes.start:off + lanes.stop]

    @pl.when(j == 0)
    def _():
        zbuf[1] = jnp.zeros(zbuf.shape[1:], f32)

    @pl.when(j <= 1)
    def _():
        for b in range(nb):
            ubuf[:, b * pu:b * pu + CONV_HALO, :] = jnp.zeros((ncc, CONV_HALO, LANES), f32)
            rbuf[:, b * pr:b * pr + LRU_HALO, :] = jnp.zeros((ncl, LRU_HALO, LANES), f32)
        hcarry[...] = jnp.zeros_like(hcarry)

    for b in range(nb):
        rs = slice(b * t_len, (b + 1) * t_len)
        for c in range(ncc):
            lanes = slice(c * LANES, (c + 1) * LANES)
            ubuf[c, b * pu + CONV_HALO:b * pu + pu, :] = (
                z_old(0, rs, lanes) * _sigmoid(z_old(wc, rs, lanes)))
        for c in range(ncl):
            lanes = slice(c * LANES, (c + 1) * LANES)
            rbuf[c, b * pr + LRU_HALO:b * pr + pr, :] = z_old(2 * wc, rs, lanes)
            r_gate = z_old(2 * wc + wl, rs, lanes)
            half_g = 0.5 * r_gate
            tanh_g = jnp.tanh(r_gate * (GELU_C0 + GELU_C1 * (r_gate * r_gate)))
            gbuf[rs, lanes] = half_g + half_g * tanh_g
        hnbuf[rs, :] = _rmsnorm(xn_ref[b], g_ref[...]).astype(jnp.bfloat16)

    def conv_and_project(i, carry):
        for sq in range(seq_per_iter):
            b = i * seq_per_iter + sq
            ubase = pl.multiple_of(b * pu, SUBLANES)
            obase = pl.multiple_of(b * t_len, SUBLANES)
            for c in range(ncc):
                lanes = slice(c * LANES, (c + 1) * LANES)
                acc = [jnp.broadcast_to(cb_ref[:, lanes], (half, LANES)) for _ in range(2)]
                for k in range(kc):
                    wk = cw_ref[k:k + 1, lanes]
                    for p in range(2):
                        off = CONV_HALO - (kc - 1) + k + p
                        acc[p] = acc[p] + wk * ubuf[c, pl.ds(ubase + off, half, stride=2), :]
                for p in range(2):
                    cbuf[c, pl.ds(obase + p, half, stride=2), :] = acc[p]
        zbuf[slot_new, i] = jnp.dot(hnbuf[...], win_ref[i], preferred_element_type=f32)
        return carry

    lax.fori_loop(0, n_proj, conv_and_project, 0)
    for b in range(nb):
        ubuf[:, b * pu:b * pu + CONV_HALO, :] = ubuf[:, b * pu + t_len:b * pu + pu, :]

    cv = jnp.concatenate([cbuf[c] for c in range(ncc)], axis=1)
    mu = jnp.mean(cv, axis=-1, keepdims=True)
    cc = cv - mu
    var = jnp.mean(cc * cc, axis=-1, keepdims=True)
    ln = cc * lax.rsqrt(var + LN_EPS) * lng_ref[...] + lnb_ref[...]
    ua = ln * _sigmoid(ln)

    for c in range(ncl):
        lanes = slice(c * LANES, (c + 1) * LANES)
        wk = [rw_ref[k:k + 1, lanes] for k in range(kr)]
        bias = rb_ref[:, lanes]
        window = [rbuf[c, pl.ds(LRU_HALO - (kr - 1) + i, nb, stride=pr), :]
                  for i in range(kr - 1)]
        for t in range(t_len):
            window.append(rbuf[c, pl.ds(LRU_HALO + t, nb, stride=pr), :])
            xr_t = bias + wk[0] * window[0]
            for k in range(1, kr):
                xr_t = xr_t + wk[k] * window[k]
            xrbuf[t * nb:(t + 1) * nb, lanes] = xr_t
            window = window[1:]
    for b in range(nb):
        rbuf[:, b * pr:b * pr + LRU_HALO, :] = rbuf[:, b * pr + t_len:b * pr + pr, :]

    xr = xrbuf[...]
    gates = jnp.dot(xr.astype(jnp.bfloat16), wgate_ref[...],
                    preferred_element_type=f32) + bgate_ref[...]
    r = _sigmoid(gates[:, 0:wl])
    i_gate = _sigmoid(gates[:, wl:])
    lam = lam_ref[...]
    softplus_neg_lam = jnp.maximum(-lam, 0.0) + jnp.log1p(jnp.exp(-jnp.abs(lam)))
    log_a = (-LRU_C) * r * softplus_neg_lam
    a = jnp.exp(log_a)
    mult = jnp.sqrt(jnp.tanh(-log_a) * (1.0 + a * a))
    abuf[...] = a
    bbuf[...] = mult * (i_gate * xr)

    h = hcarry[...]
    for t in range(t_len):
        h = abuf[t * nb:(t + 1) * nb, :] * h + bbuf[t * nb:(t + 1) * nb, :]
        for c, hc in enumerate(_lane_chunks(h)):
            hbuf[c, pl.ds(t, nb, stride=pr), :] = hc
    hcarry[...] = h

    hs = jnp.concatenate(
        [jnp.concatenate([hbuf[c, b * pr:b * pr + t_len, :] for c in range(ncl)], axis=1)
         for b in range(nb)], axis=0)
    yr = hs * gbuf[...]

    y = jnp.dot(ua.astype(jnp.bfloat16), wout_ref[0:wc, :], preferred_element_type=f32)
    y = y + jnp.dot(yr.astype(jnp.bfloat16), wout_ref[wc:, :], preferred_element_type=f32)
    o_ref[...] = (xp_ref[...].reshape(rows, d) + y).reshape(nb, t_len, d)


def _mixer(x, mix_norm, w_in, conv_dw, conv_dw_bias, conv_ln_g, conv_ln_b,
           lru_conv_w, lru_conv_b, w_gates, b_gates, lru_lambda, w_out):
    bsz, s, d = x.shape
    t_len = MIX_T
    rows = bsz * t_len
    wc = conv_dw.shape[1]
    wl = lru_conv_w.shape[1]
    assert bsz == SUBLANES and wc % LANES == 0 and wl % LANES == 0
    assert s % t_len == 0 and t_len % (2 * SUBLANES) == 0
    assert ((LRU_HALO + t_len) // SUBLANES) % 2 == 1
    f32 = jnp.float32
    consts = (mix_norm, w_in, conv_dw, conv_dw_bias, conv_ln_g, conv_ln_b,
              lru_conv_w, lru_conv_b, w_gates, b_gates, lru_lambda, w_out)
    n_tiles = s // t_len
    n_proj, _, wq = w_in.shape
    assert wc % wq == 0 or wq % wc == 0
    assert bsz % n_proj == 0
    prev_tile = lambda j: (0, jnp.maximum(j - 1, 0), 0)
    return pl.pallas_call(
        _mixer_kernel,
        grid=(n_tiles + 1,),
        in_specs=[pl.BlockSpec((bsz, t_len, d), lambda j: (0, jnp.minimum(j, n_tiles - 1), 0)),
                  pl.BlockSpec((bsz, t_len, d), prev_tile)]
        + [_const_spec(c.shape) for c in consts],
        out_specs=pl.BlockSpec((bsz, t_len, d), prev_tile),
        out_shape=jax.ShapeDtypeStruct((bsz, s, d), f32),
        scratch_shapes=[
            pltpu.VMEM((2, n_proj, rows, wq), f32),
            pltpu.VMEM((rows, d), jnp.bfloat16),
            pltpu.VMEM((rows, wl), f32),
            pltpu.VMEM((wc // LANES, bsz * (CONV_HALO + t_len), LANES), f32),
            pltpu.VMEM((wc // LANES, rows, LANES), f32),
            pltpu.VMEM((wl // LANES, bsz * (LRU_HALO + t_len), LANES), f32),
            pltpu.VMEM((rows, wl), f32),
            pltpu.VMEM((rows, wl), f32),
            pltpu.VMEM((rows, wl), f32),
            pltpu.VMEM((wl // LANES, bsz * (LRU_HALO + t_len), LANES), f32),
            pltpu.VMEM((bsz, wl), f32),
        ],
        compiler_params=pltpu.CompilerParams(
            dimension_semantics=("arbitrary",), vmem_limit_bytes=VMEM_LIMIT),
        name="hybrid_mixer",
    )(x, x, *consts)


def kernel(x, ffn1_norm, ffn1_w_gate, ffn1_w_up, ffn1_w_down, mix_norm, w_in,
           conv_dw, conv_dw_bias, conv_ln_g, conv_ln_b, lru_conv_w, lru_conv_b,
           lru_w_a, lru_b_a, lru_w_x, lru_b_x, lru_lambda, w_out,
           ffn2_norm, ffn2_w_gate, ffn2_w_up, ffn2_w_down, final_norm):
    bsz, s, d = x.shape
    bf16 = jnp.bfloat16
    row = lambda v: v.reshape(1, -1)

    w_gates = jnp.concatenate(
        [block_diag(*lru_w_a), block_diag(*lru_w_x)], axis=1).astype(bf16)
    b_gates = jnp.concatenate([lru_b_a, lru_b_x]).reshape(1, -1)

    x1 = _ffn(x.reshape(bsz * s, d), row(ffn1_norm), ffn1_w_gate.astype(bf16),
              ffn1_w_up.astype(bf16), ffn1_w_down.astype(bf16), row(final_norm),
              final_norm=False, name="ffn1")
    w_in_chunks = w_in.astype(bf16).reshape(d, MIX_PROJ_CHUNKS, -1).transpose(1, 0, 2)
    x2 = _mixer(x1.reshape(bsz, s, d), row(mix_norm), w_in_chunks,
                conv_dw, row(conv_dw_bias), row(conv_ln_g), row(conv_ln_b),
                lru_conv_w, row(lru_conv_b), w_gates, b_gates, row(lru_lambda),
                w_out.astype(bf16))
    y = _ffn(x2.reshape(bsz * s, d), row(ffn2_norm), ffn2_w_gate.astype(bf16),
             ffn2_w_up.astype(bf16), ffn2_w_down.astype(bf16), row(final_norm),
             final_norm=True, name="ffn2_final")
    return y.reshape(bsz, s, d)
```

```python
import functools

import jax
import jax.numpy as jnp
from jax import lax
from jax.experimental import pallas as pl
from jax.experimental.pallas import tpu as pltpu
from jax.scipy.linalg import block_diag

FFN_RES_SCALE = 0.5
RMS_EPS = 1e-6
LN_EPS = 1e-5
LRU_C = 8.0

SUBLANES = 8
LANES = 128
CONV_HALO = 32
LRU_HALO = 8

MXU_DEPTH = 256

FFN_TILE = 1024
FFN_CHUNKS = 2
MIX_T = 128
VMEM_LIMIT = 58 * 1024 * 1024

NEG_LOG2_E = -1.4426950408889634
GELU_C0 = 0.7978845608028654
GELU_C1 = 0.7978845608028654 * 0.044715


def _rmsnorm(x, g):
    ms = jnp.mean(x * x, axis=-1, keepdims=True)
    return x * lax.rsqrt(ms + RMS_EPS) * g


def _sigmoid(x):
    return 1.0 / (1.0 + jnp.exp2(x * NEG_LOG2_E))


def _ffn_chunks(d_ff):
    tiles = -(-d_ff // MXU_DEPTH)
    per = -(-tiles // FFN_CHUNKS)
    edges = [min(q * per * MXU_DEPTH, d_ff) for q in range(FFN_CHUNKS + 1)]
    return [(lo, hi) for lo, hi in zip(edges[:-1], edges[1:]) if hi > lo]


def _const_spec(shape):
    nd = len(shape)
    return pl.BlockSpec(shape, lambda *_: (0,) * nd, pipeline_mode=pl.Buffered(1))


def _ffn_kernel(x_ref, g_ref, wg_ref, wu_ref, wd_ref, fg_ref, o_ref, *, final_norm):
    x = x_ref[...]
    xn = _rmsnorm(x, g_ref[...]).astype(jnp.bfloat16)
    d_ff = wg_ref.shape[1]
    y = None
    for lo, hi in _ffn_chunks(d_ff):
        gate = jnp.dot(xn, wg_ref[:, lo:hi], preferred_element_type=jnp.float32)
        up = jnp.dot(xn, wu_ref[:, lo:hi], preferred_element_type=jnp.float32)
        act = (gate * _sigmoid(gate) * up).astype(jnp.bfloat16)
        yq = jnp.dot(act, wd_ref[lo:hi, :], preferred_element_type=jnp.float32)
        y = yq if y is None else y + yq
    out = x + FFN_RES_SCALE * y
    if final_norm:
        out = _rmsnorm(out, fg_ref[...])
    o_ref[...] = out


def _ffn(x2d, norm_g, w_gate, w_up, w_down, final_g, *, final_norm, name):
    n, d = x2d.shape
    d_ff = w_gate.shape[1]
    tm = FFN_TILE
    return pl.pallas_call(
        functools.partial(_ffn_kernel, final_norm=final_norm),
        grid=(n // tm,),
        in_specs=[
            pl.BlockSpec((tm, d), lambda i: (i, 0)),
            _const_spec((1, d)),
            _const_spec((d, d_ff)),
            _const_spec((d, d_ff)),
            _const_spec((d_ff, d)),
            _const_spec((1, d)),
        ],
        out_specs=pl.BlockSpec((tm, d), lambda i: (i, 0)),
        out_shape=jax.ShapeDtypeStruct((n, d), jnp.float32),
        compiler_params=pltpu.CompilerParams(
            dimension_semantics=("arbitrary",), vmem_limit_bytes=VMEM_LIMIT),
        name=name,
    )(x2d, norm_g, w_gate, w_up, w_down, final_g)


def _lane_chunks(v):
    return [v[:, c * LANES:(c + 1) * LANES] for c in range(v.shape[1] // LANES)]


def _mixer_kernel(x_ref, g_ref, win_ref, cw_ref, cb_ref, lng_ref, lnb_ref,
                  rw_ref, rb_ref, wgate_ref, bgate_ref, lam_ref, wout_ref,
                  o_ref,
                  ubuf, cbuf, rbuf, xrbuf, abuf, bbuf, hbuf, hcarry):
    nb, t_len, d = x_ref.shape
    rows = nb * t_len
    wc = cw_ref.shape[1]
    wl = rw_ref.shape[1]
    kc = cw_ref.shape[0]
    kr = rw_ref.shape[0]
    ncc = wc // LANES
    ncl = wl // LANES
    pu = CONV_HALO + t_len
    pr = LRU_HALO + t_len
    half = t_len // 2
    f32 = jnp.float32

    @pl.when(pl.program_id(0) == 0)
    def _():
        for b in range(nb):
            ubuf[:, b * pu:b * pu + CONV_HALO, :] = jnp.zeros((ncc, CONV_HALO, LANES), f32)
            rbuf[:, b * pr:b * pr + LRU_HALO, :] = jnp.zeros((ncl, LRU_HALO, LANES), f32)
        hcarry[...] = jnp.zeros_like(hcarry)

    x = x_ref[...].reshape(rows, d)
    hn = _rmsnorm(x, g_ref[...]).astype(jnp.bfloat16)
    z = jnp.dot(hn, win_ref[...], preferred_element_type=f32)
    c_val = z[:, 0:wc]
    c_gate = z[:, wc:2 * wc]
    r_x = z[:, 2 * wc:2 * wc + wl]
    r_gate = z[:, 2 * wc + wl:]

    u = c_val * _sigmoid(c_gate)
    for c, uc in enumerate(_lane_chunks(u)):
        for b in range(nb):
            ubuf[c, b * pu + CONV_HALO:b * pu + pu, :] = uc[b * t_len:(b + 1) * t_len, :]
    for c, rc in enumerate(_lane_chunks(r_x)):
        for b in range(nb):
            rbuf[c, b * pr + LRU_HALO:b * pr + pr, :] = rc[b * t_len:(b + 1) * t_len, :]

    def conv_seq(b, carry):
        ubase = pl.multiple_of(b * pu, SUBLANES)
        obase = pl.multiple_of(b * t_len, SUBLANES)
        for c in range(ncc):
            lanes = slice(c * LANES, (c + 1) * LANES)
            acc = [jnp.broadcast_to(cb_ref[:, lanes], (half, LANES)) for _ in range(2)]
            for k in range(kc):
                wk = cw_ref[k:k + 1, lanes]
                for p in range(2):
                    off = CONV_HALO - (kc - 1) + k + p
                    acc[p] = acc[p] + wk * ubuf[c, pl.ds(ubase + off, half, stride=2), :]
            for p in range(2):
                cbuf[c, pl.ds(obase + p, half, stride=2), :] = acc[p]
        return carry

    lax.fori_loop(0, nb, conv_seq, 0)
    for b in range(nb):
        ubuf[:, b * pu:b * pu + CONV_HALO, :] = ubuf[:, b * pu + t_len:b * pu + pu, :]

    cv = jnp.concatenate([cbuf[c] for c in range(ncc)], axis=1)
    mu = jnp.mean(cv, axis=-1, keepdims=True)
    cc = cv - mu
    var = jnp.mean(cc * cc, axis=-1, keepdims=True)
    ln = cc * lax.rsqrt(var + LN_EPS) * lng_ref[...] + lnb_ref[...]
    ua = ln * _sigmoid(ln)

    for c in range(ncl):
        lanes = slice(c * LANES, (c + 1) * LANES)
        wk = [rw_ref[k:k + 1, lanes] for k in range(kr)]
        bias = rb_ref[:, lanes]
        window = [rbuf[c, pl.ds(LRU_HALO - (kr - 1) + i, nb, stride=pr), :]
                  for i in range(kr - 1)]
        for t in range(t_len):
            window.append(rbuf[c, pl.ds(LRU_HALO + t, nb, stride=pr), :])
            xr_t = bias + wk[0] * window[0]
            for k in range(1, kr):
                xr_t = xr_t + wk[k] * window[k]
            xrbuf[t * nb:(t + 1) * nb, lanes] = xr_t
            window = window[1:]
    for b in range(nb):
        rbuf[:, b * pr:b * pr + LRU_HALO, :] = rbuf[:, b * pr + t_len:b * pr + pr, :]

    xr = xrbuf[...]
    n_blk = wgate_ref.shape[0]
    wb = wl // n_blk
    g_blk = [jnp.dot(xr[:, q * wb:(q + 1) * wb].astype(jnp.bfloat16), wgate_ref[q],
                     preferred_element_type=f32) for q in range(n_blk)]
    bg = bgate_ref[...]
    r = _sigmoid(jnp.concatenate([g[:, 0:wb] for g in g_blk], axis=1) + bg[:, 0:wl])
    i_gate = _sigmoid(jnp.concatenate([g[:, wb:] for g in g_blk], axis=1) + bg[:, wl:])
    lam = lam_ref[...]
    softplus_neg_lam = jnp.maximum(-lam, 0.0) + jnp.log1p(jnp.exp(-jnp.abs(lam)))
    log_a = (-LRU_C) * r * softplus_neg_lam
    a = jnp.exp(log_a)
    m2 = jnp.tanh(-log_a) * (1.0 + a * a)
    mult = jnp.where(m2 > 0.0, m2 * lax.rsqrt(m2), 0.0)
    abuf[...] = a
    bbuf[...] = mult * (i_gate * xr)

    h = hcarry[...]
    for t in range(t_len):
        h = abuf[t * nb:(t + 1) * nb, :] * h + bbuf[t * nb:(t + 1) * nb, :]
        for c, hc in enumerate(_lane_chunks(h)):
            hbuf[c, pl.ds(t, nb, stride=pr), :] = hc
    hcarry[...] = h

    hs = jnp.concatenate(
        [jnp.concatenate([hbuf[c, b * pr:b * pr + t_len, :] for c in range(ncl)], axis=1)
         for b in range(nb)], axis=0)
    half_g = 0.5 * r_gate
    tanh_g = jnp.tanh(r_gate * (GELU_C0 + GELU_C1 * (r_gate * r_gate)))
    yr = hs * (half_g + half_g * tanh_g)

    y = jnp.dot(ua.astype(jnp.bfloat16), wout_ref[0:wc, :], preferred_element_type=f32)
    y = y + jnp.dot(yr.astype(jnp.bfloat16), wout_ref[wc:, :], preferred_element_type=f32)
    o_ref[...] = (x + y).reshape(nb, t_len, d)


def _mixer(x, mix_norm, w_in, conv_dw, conv_dw_bias, conv_ln_g, conv_ln_b,
           lru_conv_w, lru_conv_b, w_gates, b_gates, lru_lambda, w_out):
    bsz, s, d = x.shape
    t_len = MIX_T
    rows = bsz * t_len
    wc = conv_dw.shape[1]
    wl = lru_conv_w.shape[1]
    assert bsz == SUBLANES and wc % LANES == 0 and wl % LANES == 0
    assert s % t_len == 0 and t_len % (2 * SUBLANES) == 0
    assert ((LRU_HALO + t_len) // SUBLANES) % 2 == 1
    f32 = jnp.float32
    consts = (mix_norm, w_in, conv_dw, conv_dw_bias, conv_ln_g, conv_ln_b,
              lru_conv_w, lru_conv_b, w_gates, b_gates, lru_lambda, w_out)
    return pl.pallas_call(
        _mixer_kernel,
        grid=(s // t_len,),
        in_specs=[pl.BlockSpec((bsz, t_len, d), lambda j: (0, j, 0))]
        + [_const_spec(c.shape) for c in consts],
        out_specs=pl.BlockSpec((bsz, t_len, d), lambda j: (0, j, 0)),
        out_shape=jax.ShapeDtypeStruct((bsz, s, d), f32),
        scratch_shapes=[
            pltpu.VMEM((wc // LANES, bsz * (CONV_HALO + t_len), LANES), f32),
            pltpu.VMEM((wc // LANES, rows, LANES), f32),
            pltpu.VMEM((wl // LANES, bsz * (LRU_HALO + t_len), LANES), f32),
            pltpu.VMEM((rows, wl), f32),
            pltpu.VMEM((rows, wl), f32),
            pltpu.VMEM((rows, wl), f32),
            pltpu.VMEM((wl // LANES, bsz * (LRU_HALO + t_len), LANES), f32),
            pltpu.VMEM((bsz, wl), f32),
        ],
        compiler_params=pltpu.CompilerParams(
            dimension_semantics=("arbitrary",), vmem_limit_bytes=VMEM_LIMIT),
        name="hybrid_mixer",
    )(x, *consts)


def kernel(x, ffn1_norm, ffn1_w_gate, ffn1_w_up, ffn1_w_down, mix_norm, w_in,
           conv_dw, conv_dw_bias, conv_ln_g, conv_ln_b, lru_conv_w, lru_conv_b,
           lru_w_a, lru_b_a, lru_w_x, lru_b_x, lru_lambda, w_out,
           ffn2_norm, ffn2_w_gate, ffn2_w_up, ffn2_w_down, final_norm):
    bsz, s, d = x.shape
    bf16 = jnp.bfloat16
    row = lambda v: v.reshape(1, -1)

    n_heads, head_dim, _ = lru_w_a.shape
    per_blk = MXU_DEPTH // head_dim
    w_gates = jnp.stack([
        jnp.concatenate([block_diag(*lru_w_a[q:q + per_blk]),
                         block_diag(*lru_w_x[q:q + per_blk])], axis=1)
        for q in range(0, n_heads, per_blk)]).astype(bf16)
    b_gates = jnp.concatenate([lru_b_a, lru_b_x]).reshape(1, -1)

    x1 = _ffn(x.reshape(bsz * s, d), row(ffn1_norm), ffn1_w_gate.astype(bf16),
              ffn1_w_up.astype(bf16), ffn1_w_down.astype(bf16), row(final_norm),
              final_norm=False, name="ffn1")
    x2 = _mixer(x1.reshape(bsz, s, d), row(mix_norm), w_in.astype(bf16),
                conv_dw, row(conv_dw_bias), row(conv_ln_g), row(conv_ln_b),
                lru_conv_w, row(lru_conv_b), w_gates, b_gates, row(lru_lambda),
                w_out.astype(bf16))
    y = _ffn(x2.reshape(bsz * s, d), row(ffn2_norm), ffn2_w_gate.astype(bf16),
             ffn2_w_up.astype(bf16), ffn2_w_down.astype(bf16), row(final_norm),
             final_norm=True, name="ffn2_final")
    return y.reshape(bsz, s, d)
```

```python
import functools

import jax
import jax.numpy as jnp
from jax import lax
from jax.experimental import pallas as pl
from jax.experimental.pallas import tpu as pltpu
from jax.scipy.linalg import block_diag

FFN_RES_SCALE = 0.5
RMS_EPS = 1e-6
LN_EPS = 1e-5
LRU_C = 8.0

SUBLANES = 8
LANES = 128
BF16_TILE_ROWS = 16
CONV_HALO = 32
LRU_HALO = 8

MXU_DEPTH = 256

FFN_TILE = 1024
FFN_CHUNKS = 2
MIX_T = 128
VMEM_LIMIT = 58 * 1024 * 1024

NEG_LOG2_E = -1.4426950408889634
GELU_C0 = 0.7978845608028654
GELU_C1 = 0.7978845608028654 * 0.044715


def _rmsnorm(x, g):
    ms = jnp.mean(x * x, axis=-1, keepdims=True)
    return x * lax.rsqrt(ms + RMS_EPS) * g


def _sigmoid(x):
    return 1.0 / (1.0 + jnp.exp2(x * NEG_LOG2_E))


def _ffn_chunks(d_ff):
    tiles = -(-d_ff // MXU_DEPTH)
    per = -(-tiles // FFN_CHUNKS)
    edges = [min(q * per * MXU_DEPTH, d_ff) for q in range(FFN_CHUNKS + 1)]
    return [(lo, hi) for lo, hi in zip(edges[:-1], edges[1:]) if hi > lo]


def _const_spec(shape):
    nd = len(shape)
    return pl.BlockSpec(shape, lambda *_: (0,) * nd, pipeline_mode=pl.Buffered(1))


def _ffn_kernel(x_ref, g_ref, wg_ref, wu_ref, wd_ref, fg_ref, o_ref, *, final_norm):
    x = x_ref[...]
    xn = _rmsnorm(x, g_ref[...]).astype(jnp.bfloat16)
    d_ff = wg_ref.shape[1]
    y = None
    for lo, hi in _ffn_chunks(d_ff):
        gate = jnp.dot(xn, wg_ref[:, lo:hi], preferred_element_type=jnp.float32)
        up = jnp.dot(xn, wu_ref[:, lo:hi], preferred_element_type=jnp.float32)
        act = (gate * _sigmoid(gate) * up).astype(jnp.bfloat16)
        yq = jnp.dot(act, wd_ref[lo:hi, :], preferred_element_type=jnp.float32)
        y = yq if y is None else y + yq
    out = x + FFN_RES_SCALE * y
    if final_norm:
        out = _rmsnorm(out, fg_ref[...])
    o_ref[...] = out


def _ffn(x2d, norm_g, w_gate, w_up, w_down, final_g, *, final_norm, name):
    n, d = x2d.shape
    d_ff = w_gate.shape[1]
    tm = FFN_TILE
    return pl.pallas_call(
        functools.partial(_ffn_kernel, final_norm=final_norm),
        grid=(n // tm,),
        in_specs=[
            pl.BlockSpec((tm, d), lambda i: (i, 0)),
            _const_spec((1, d)),
            _const_spec((d, d_ff)),
            _const_spec((d, d_ff)),
            _const_spec((d_ff, d)),
            _const_spec((1, d)),
        ],
        out_specs=pl.BlockSpec((tm, d), lambda i: (i, 0)),
        out_shape=jax.ShapeDtypeStruct((n, d), jnp.float32),
        compiler_params=pltpu.CompilerParams(
            dimension_semantics=("arbitrary",), vmem_limit_bytes=VMEM_LIMIT),
        name=name,
    )(x2d, norm_g, w_gate, w_up, w_down, final_g)


def _lane_chunks(v):
    return [v[:, c * LANES:(c + 1) * LANES] for c in range(v.shape[1] // LANES)]


def _mixer_kernel(x_ref, g_ref, win_ref, cw_ref, cb_ref, lng_ref, lnb_ref,
                  rw_ref, rb_ref, wgate_ref, bgate_ref, lam_ref, wout_ref,
                  cast_a_ref, cast_b_ref, cast_c_ref,
                  o_ref, cast_a_out, cast_b_out, cast_c_out,
                  ubuf, cbuf, rbuf, xrbuf, abuf, bbuf, hbuf, hcarry, *, cast_blocks):
    nb, t_len, d = x_ref.shape
    rows = nb * t_len
    wc = cw_ref.shape[1]
    wl = rw_ref.shape[1]
    kc = cw_ref.shape[0]
    kr = rw_ref.shape[0]
    ncc = wc // LANES
    ncl = wl // LANES
    pu = CONV_HALO + t_len
    pr = LRU_HALO + t_len
    half = t_len // 2
    f32 = jnp.float32

    for src, dst, n_blk in zip((cast_a_ref, cast_b_ref, cast_c_ref),
                               (cast_a_out, cast_b_out, cast_c_out), cast_blocks):
        if n_blk == pl.num_programs(0):
            dst[...] = src[...].astype(jnp.bfloat16)
        else:
            @pl.when(pl.program_id(0) < n_blk)
            def _(src=src, dst=dst):
                dst[...] = src[...].astype(jnp.bfloat16)

    @pl.when(pl.program_id(0) == 0)
    def _():
        for b in range(nb):
            ubuf[:, b * pu:b * pu + CONV_HALO, :] = jnp.zeros((ncc, CONV_HALO, LANES), f32)
            rbuf[:, b * pr:b * pr + LRU_HALO, :] = jnp.zeros((ncl, LRU_HALO, LANES), f32)
        hcarry[...] = jnp.zeros_like(hcarry)

    x = x_ref[...].reshape(rows, d)
    hn = _rmsnorm(x, g_ref[...]).astype(jnp.bfloat16)
    z = jnp.dot(hn, win_ref[...], preferred_element_type=f32)
    c_val = z[:, 0:wc]
    c_gate = z[:, wc:2 * wc]
    r_x = z[:, 2 * wc:2 * wc + wl]
    r_gate = z[:, 2 * wc + wl:]

    u = c_val * _sigmoid(c_gate)
    for c, uc in enumerate(_lane_chunks(u)):
        for b in range(nb):
            ubuf[c, b * pu + CONV_HALO:b * pu + pu, :] = uc[b * t_len:(b + 1) * t_len, :]
    for c, rc in enumerate(_lane_chunks(r_x)):
        for b in range(nb):
            rbuf[c, b * pr + LRU_HALO:b * pr + pr, :] = rc[b * t_len:(b + 1) * t_len, :]

    def conv_seq(b, carry):
        ubase = pl.multiple_of(b * pu, SUBLANES)
        obase = pl.multiple_of(b * t_len, SUBLANES)
        for c in range(ncc):
            lanes = slice(c * LANES, (c + 1) * LANES)
            acc = [jnp.broadcast_to(cb_ref[:, lanes], (half, LANES)) for _ in range(2)]
            for m in range(kc + 1):
                um = ubuf[c, pl.ds(ubase + CONV_HALO - (kc - 1) + m, half, stride=2), :]
                if m < kc:
                    acc[0] = acc[0] + cw_ref[m:m + 1, lanes] * um
                if m >= 1:
                    acc[1] = acc[1] + cw_ref[m - 1:m, lanes] * um
            for p in range(2):
                cbuf[c, pl.ds(obase + p, half, stride=2), :] = acc[p]
        return carry

    lax.fori_loop(0, nb, conv_seq, 0)
    for b in range(nb):
        ubuf[:, b * pu:b * pu + CONV_HALO, :] = ubuf[:, b * pu + t_len:b * pu + pu, :]

    cv = jnp.concatenate([cbuf[c] for c in range(ncc)], axis=1)
    mu = jnp.mean(cv, axis=-1, keepdims=True)
    cc = cv - mu
    var = jnp.mean(cc * cc, axis=-1, keepdims=True)
    ln = cc * lax.rsqrt(var + LN_EPS) * lng_ref[...] + lnb_ref[...]
    ua = ln * _sigmoid(ln)

    for c in range(ncl):
        lanes = slice(c * LANES, (c + 1) * LANES)
        wk = [rw_ref[k:k + 1, lanes] for k in range(kr)]
        bias = rb_ref[:, lanes]
        window = [rbuf[c, pl.ds(LRU_HALO - (kr - 1) + i, nb, stride=pr), :]
                  for i in range(kr - 1)]
        for t in range(t_len):
            window.append(rbuf[c, pl.ds(LRU_HALO + t, nb, stride=pr), :])
            xr_t = bias + wk[0] * window[0]
            for k in range(1, kr):
                xr_t = xr_t + wk[k] * window[k]
            xrbuf[t * nb:(t + 1) * nb, lanes] = xr_t
            window = window[1:]
    for b in range(nb):
        rbuf[:, b * pr:b * pr + LRU_HALO, :] = rbuf[:, b * pr + t_len:b * pr + pr, :]

    xr = xrbuf[...]
    n_blk = wgate_ref.shape[0]
    wb = wl // n_blk
    g_blk = [jnp.dot(xr[:, q * wb:(q + 1) * wb].astype(jnp.bfloat16), wgate_ref[q],
                     preferred_element_type=f32) for q in range(n_blk)]
    bg = bgate_ref[...]
    r = _sigmoid(jnp.concatenate([g[:, 0:wb] for g in g_blk], axis=1) + bg[:, 0:wl])
    i_gate = _sigmoid(jnp.concatenate([g[:, wb:] for g in g_blk], axis=1) + bg[:, wl:])
    lam = lam_ref[...]
    softplus_neg_lam = jnp.maximum(-lam, 0.0) + jnp.log1p(jnp.exp(-jnp.abs(lam)))
    k_neg = LRU_C * softplus_neg_lam
    a = jnp.exp2(r * (k_neg * NEG_LOG2_E))
    m2 = jnp.tanh(r * k_neg) * (1.0 + a * a)
    mult = jnp.where(m2 > 0.0, m2 * lax.rsqrt(m2), 0.0)
    abuf[...] = a
    bbuf[...] = mult * (i_gate * xr)

    h = hcarry[...]
    for t in range(t_len):
        h = abuf[t * nb:(t + 1) * nb, :] * h + bbuf[t * nb:(t + 1) * nb, :]
        for c, hc in enumerate(_lane_chunks(h)):
            hbuf[c, pl.ds(t, nb, stride=pr), :] = hc
    hcarry[...] = h

    hs = jnp.concatenate(
        [jnp.concatenate([hbuf[c, b * pr:b * pr + t_len, :] for c in range(ncl)], axis=1)
         for b in range(nb)], axis=0)
    half_g = 0.5 * r_gate
    tanh_g = jnp.tanh(r_gate * (GELU_C0 + GELU_C1 * (r_gate * r_gate)))
    yr = hs * (half_g + half_g * tanh_g)

    y = jnp.dot(ua.astype(jnp.bfloat16), wout_ref[0:wc, :], preferred_element_type=f32)
    y = y + jnp.dot(yr.astype(jnp.bfloat16), wout_ref[wc:, :], preferred_element_type=f32)
    o_ref[...] = (x + y).reshape(nb, t_len, d)


def _cast_rows(n_rows, n_steps):
    for rows_blk in range(BF16_TILE_ROWS, n_rows + 1, BF16_TILE_ROWS):
        if n_rows % rows_blk == 0 and n_rows // rows_blk <= n_steps:
            return rows_blk
    raise ValueError((n_rows, n_steps))


def _mixer(x, mix_norm, w_in, conv_dw, conv_dw_bias, conv_ln_g, conv_ln_b,
           lru_conv_w, lru_conv_b, w_gates, b_gates, lru_lambda, w_out, to_cast):
    bsz, s, d = x.shape
    t_len = MIX_T
    rows = bsz * t_len
    wc = conv_dw.shape[1]
    wl = lru_conv_w.shape[1]
    assert bsz == SUBLANES and wc % LANES == 0 and wl % LANES == 0
    assert s % t_len == 0 and t_len % (2 * SUBLANES) == 0
    assert ((LRU_HALO + t_len) // SUBLANES) % 2 == 1
    f32 = jnp.float32
    consts = (mix_norm, w_in, conv_dw, conv_dw_bias, conv_ln_g, conv_ln_b,
              lru_conv_w, lru_conv_b, w_gates, b_gates, lru_lambda, w_out)
    n_steps = s // t_len
    cast_specs, cast_blocks = [], []
    for w in to_cast:
        rows_blk = _cast_rows(w.shape[0], n_steps)
        n_blk = w.shape[0] // rows_blk
        cast_blocks.append(n_blk)
        cast_specs.append(pl.BlockSpec(
            (rows_blk, w.shape[1]), lambda j, n_blk=n_blk: (jnp.minimum(j, n_blk - 1), 0)))
    return pl.pallas_call(
        functools.partial(_mixer_kernel, cast_blocks=tuple(cast_blocks)),
        grid=(n_steps,),
        in_specs=[pl.BlockSpec((bsz, t_len, d), lambda j: (0, j, 0))]
        + [_const_spec(c.shape) for c in consts] + cast_specs,
        out_specs=[pl.BlockSpec((bsz, t_len, d), lambda j: (0, j, 0))] + cast_specs,
        out_shape=[jax.ShapeDtypeStruct((bsz, s, d), f32)]
        + [jax.ShapeDtypeStruct(w.shape, jnp.bfloat16) for w in to_cast],
        scratch_shapes=[
            pltpu.VMEM((wc // LANES, bsz * (CONV_HALO + t_len), LANES), f32),
            pltpu.VMEM((wc // LANES, rows, LANES), f32),
            pltpu.VMEM((wl // LANES, bsz * (LRU_HALO + t_len), LANES), f32),
            pltpu.VMEM((rows, wl), f32),
            pltpu.VMEM((rows, wl), f32),
            pltpu.VMEM((rows, wl), f32),
            pltpu.VMEM((wl // LANES, bsz * (LRU_HALO + t_len), LANES), f32),
            pltpu.VMEM((bsz, wl), f32),
        ],
        compiler_params=pltpu.CompilerParams(
            dimension_semantics=("arbitrary",), vmem_limit_bytes=VMEM_LIMIT),
        name="hybrid_mixer",
    )(x, *consts, *to_cast)


def kernel(x, ffn1_norm, ffn1_w_gate, ffn1_w_up, ffn1_w_down, mix_norm, w_in,
           conv_dw, conv_dw_bias, conv_ln_g, conv_ln_b, lru_conv_w, lru_conv_b,
           lru_w_a, lru_b_a, lru_w_x, lru_b_x, lru_lambda, w_out,
           ffn2_norm, ffn2_w_gate, ffn2_w_up, ffn2_w_down, final_norm):
    bsz, s, d = x.shape
    bf16 = jnp.bfloat16
    row = lambda v: v.reshape(1, -1)

    n_heads, head_dim, _ = lru_w_a.shape
    per_blk = MXU_DEPTH // head_dim
    w_gates = jnp.stack([
        jnp.concatenate([block_diag(*lru_w_a[q:q + per_blk]),
                         block_diag(*lru_w_x[q:q + per_blk])], axis=1)
        for q in range(0, n_heads, per_blk)]).astype(bf16)
    b_gates = jnp.concatenate([lru_b_a, lru_b_x]).reshape(1, -1)

    x1 = _ffn(x.reshape(bsz * s, d), row(ffn1_norm), ffn1_w_gate.astype(bf16),
              ffn1_w_up.astype(bf16), ffn1_w_down.astype(bf16), row(final_norm),
              final_norm=False, name="ffn1")
    x2, w_gate2, w_up2, w_down2 = _mixer(
        x1.reshape(bsz, s, d), row(mix_norm), w_in.astype(bf16),
        conv_dw, row(conv_dw_bias), row(conv_ln_g), row(conv_ln_b),
        lru_conv_w, row(lru_conv_b), w_gates, b_gates, row(lru_lambda),
        w_out.astype(bf16), (ffn2_w_gate, ffn2_w_up, ffn2_w_down))
    y = _ffn(x2.reshape(bsz * s, d), row(ffn2_norm), w_gate2, w_up2, w_down2,
             row(final_norm), final_norm=True, name="ffn2_final")
    return y.reshape(bsz, s, d)
```

```python
import functools

import jax
import jax.numpy as jnp
from jax import lax
from jax.experimental import pallas as pl
from jax.experimental.pallas import tpu as pltpu
from jax.scipy.linalg import block_diag

FFN_RES_SCALE = 0.5
RMS_EPS = 1e-6
LN_EPS = 1e-5
LRU_C = 8.0

SUBLANES = 8
LANES = 128
BF16_TILE_ROWS = 16
CONV_HALO = 32
LRU_HALO = 8

MXU_DEPTH = 256

FFN_TILE = 1024
FFN_CHUNKS = 2
MIX_T = 128
VMEM_LIMIT = 58 * 1024 * 1024

NEG_LOG2_E = -1.4426950408889634
GELU_C0 = 0.7978845608028654
GELU_C1 = 0.7978845608028654 * 0.044715


def _rmsnorm(x, g):
    ms = jnp.mean(x * x, axis=-1, keepdims=True)
    return x * lax.rsqrt(ms + RMS_EPS) * g


def _sigmoid(x):
    return 1.0 / (1.0 + jnp.exp2(x * NEG_LOG2_E))


def _ffn_chunks(d_ff):
    tiles = -(-d_ff // MXU_DEPTH)
    per = -(-tiles // FFN_CHUNKS)
    edges = [min(q * per * MXU_DEPTH, d_ff) for q in range(FFN_CHUNKS + 1)]
    return [(lo, hi) for lo, hi in zip(edges[:-1], edges[1:]) if hi > lo]


def _const_spec(shape):
    nd = len(shape)
    return pl.BlockSpec(shape, lambda *_: (0,) * nd, pipeline_mode=pl.Buffered(1))


def _ffn_kernel(x_ref, g_ref, wg_ref, wu_ref, wd_ref, fg_ref, o_ref, *, final_norm):
    x = x_ref[...]
    xn = _rmsnorm(x, g_ref[...]).astype(jnp.bfloat16)
    d_ff = wg_ref.shape[1]
    y = None
    for lo, hi in _ffn_chunks(d_ff):
        gate = jnp.dot(xn, wg_ref[:, lo:hi], preferred_element_type=jnp.float32)
        up = jnp.dot(xn, wu_ref[:, lo:hi], preferred_element_type=jnp.float32)
        act = (gate * _sigmoid(gate) * up).astype(jnp.bfloat16)
        yq = jnp.dot(act, wd_ref[lo:hi, :], preferred_element_type=jnp.float32)
        y = yq if y is None else y + yq
    out = x + FFN_RES_SCALE * y
    if final_norm:
        out = _rmsnorm(out, fg_ref[...])
    o_ref[...] = out


def _ffn(x2d, norm_g, w_gate, w_up, w_down, final_g, *, final_norm, name):
    n, d = x2d.shape
    d_ff = w_gate.shape[1]
    tm = FFN_TILE
    return pl.pallas_call(
        functools.partial(_ffn_kernel, final_norm=final_norm),
        grid=(n // tm,),
        in_specs=[
            pl.BlockSpec((tm, d), lambda i: (i, 0)),
            _const_spec((1, d)),
            _const_spec((d, d_ff)),
            _const_spec((d, d_ff)),
            _const_spec((d_ff, d)),
            _const_spec((1, d)),
        ],
        out_specs=pl.BlockSpec((tm, d), lambda i: (i, 0)),
        out_shape=jax.ShapeDtypeStruct((n, d), jnp.float32),
        compiler_params=pltpu.CompilerParams(
            dimension_semantics=("arbitrary",), vmem_limit_bytes=VMEM_LIMIT),
        name=name,
    )(x2d, norm_g, w_gate, w_up, w_down, final_g)


def _lane_chunks(v):
    return [v[:, c * LANES:(c + 1) * LANES] for c in range(v.shape[1] // LANES)]


def _mixer_kernel(x_ref, g_ref, win_ref, cw_ref, cb_ref, lng_ref, lnb_ref,
                  rw_ref, rb_ref, wgate_ref, bgate_ref, lam_ref, wout_ref,
                  cast_a_ref, cast_b_ref, cast_c_ref,
                  o_ref, cast_a_out, cast_b_out, cast_c_out,
                  gbuf, ubuf, cbuf, rbuf, xrbuf, abuf, bbuf, hbuf, hcarry, *, cast_blocks):
    nb, t_len, d = x_ref.shape
    rows = nb * t_len
    wc = cw_ref.shape[1]
    wl = rw_ref.shape[1]
    kc = cw_ref.shape[0]
    kr = rw_ref.shape[0]
    ncc = wc // LANES
    ncl = wl // LANES
    pu = CONV_HALO + t_len
    pr = LRU_HALO + t_len
    half = t_len // 2
    f32 = jnp.float32

    for src, dst, n_blk in zip((cast_a_ref, cast_b_ref, cast_c_ref),
                               (cast_a_out, cast_b_out, cast_c_out), cast_blocks):
        if n_blk == pl.num_programs(0):
            dst[...] = src[...].astype(jnp.bfloat16)
        else:
            @pl.when(pl.program_id(0) < n_blk)
            def _(src=src, dst=dst):
                dst[...] = src[...].astype(jnp.bfloat16)

    @pl.when(pl.program_id(0) == 0)
    def _():
        for b in range(nb):
            ubuf[:, b * pu:b * pu + CONV_HALO, :] = jnp.zeros((ncc, CONV_HALO, LANES), f32)
            rbuf[:, b * pr:b * pr + LRU_HALO, :] = jnp.zeros((ncl, LRU_HALO, LANES), f32)
        hcarry[...] = jnp.zeros_like(hcarry)

    x = x_ref[...].reshape(rows, d)
    hn = _rmsnorm(x, g_ref[...]).astype(jnp.bfloat16)

    def project(col0, width):
        return jnp.dot(hn, win_ref[:, col0:col0 + width], preferred_element_type=f32)

    r_gate = project(2 * wc + wl, wl)
    half_g = 0.5 * r_gate
    tanh_g = jnp.tanh(r_gate * (GELU_C0 + GELU_C1 * (r_gate * r_gate)))
    gbuf[...] = half_g + half_g * tanh_g

    c_gate = project(wc, wc)
    c_val = project(0, wc)
    u = c_val * _sigmoid(c_gate)
    for c, uc in enumerate(_lane_chunks(u)):
        for b in range(nb):
            ubuf[c, b * pu + CONV_HALO:b * pu + pu, :] = uc[b * t_len:(b + 1) * t_len, :]
    r_x = project(2 * wc, wl)
    for c, rc in enumerate(_lane_chunks(r_x)):
        for b in range(nb):
            rbuf[c, b * pr + LRU_HALO:b * pr + pr, :] = rc[b * t_len:(b + 1) * t_len, :]

    def conv_seq(b, carry):
        ubase = pl.multiple_of(b * pu, SUBLANES)
        obase = pl.multiple_of(b * t_len, SUBLANES)
        for c in range(ncc):
            lanes = slice(c * LANES, (c + 1) * LANES)
            acc = [jnp.broadcast_to(cb_ref[:, lanes], (half, LANES)) for _ in range(2)]
            for m in range(kc + 1):
                um = ubuf[c, pl.ds(ubase + CONV_HALO - (kc - 1) + m, half, stride=2), :]
                if m < kc:
                    acc[0] = acc[0] + cw_ref[m:m + 1, lanes] * um
                if m >= 1:
                    acc[1] = acc[1] + cw_ref[m - 1:m, lanes] * um
            for p in range(2):
                cbuf[c, pl.ds(obase + p, half, stride=2), :] = acc[p]
        return carry

    lax.fori_loop(0, nb, conv_seq, 0)
    for b in range(nb):
        ubuf[:, b * pu:b * pu + CONV_HALO, :] = ubuf[:, b * pu + t_len:b * pu + pu, :]

    cv = jnp.concatenate([cbuf[c] for c in range(ncc)], axis=1)
    mu = jnp.mean(cv, axis=-1, keepdims=True)
    cc = cv - mu
    var = jnp.mean(cc * cc, axis=-1, keepdims=True)
    ln = cc * lax.rsqrt(var + LN_EPS) * lng_ref[...] + lnb_ref[...]
    ua = ln * _sigmoid(ln)

    for c in range(ncl):
        lanes = slice(c * LANES, (c + 1) * LANES)
        wk = [rw_ref[k:k + 1, lanes] for k in range(kr)]
        bias = rb_ref[:, lanes]
        window = [rbuf[c, pl.ds(LRU_HALO - (kr - 1) + i, nb, stride=pr), :]
                  for i in range(kr - 1)]
        for t in range(t_len):
            window.append(rbuf[c, pl.ds(LRU_HALO + t, nb, stride=pr), :])
            xr_t = bias + wk[0] * window[0]
            for k in range(1, kr):
                xr_t = xr_t + wk[k] * window[k]
            xrbuf[t * nb:(t + 1) * nb, lanes] = xr_t
            window = window[1:]
    for b in range(nb):
        rbuf[:, b * pr:b * pr + LRU_HALO, :] = rbuf[:, b * pr + t_len:b * pr + pr, :]

    xr = xrbuf[...]
    n_blk = wgate_ref.shape[0]
    wb = wl // n_blk
    g_blk = [jnp.dot(xr[:, q * wb:(q + 1) * wb].astype(jnp.bfloat16), wgate_ref[q],
                     preferred_element_type=f32) for q in range(n_blk)]
    bg = bgate_ref[...]
    r = _sigmoid(jnp.concatenate([g[:, 0:wb] for g in g_blk], axis=1) + bg[:, 0:wl])
    i_gate = _sigmoid(jnp.concatenate([g[:, wb:] for g in g_blk], axis=1) + bg[:, wl:])
    lam = lam_ref[...]
    softplus_neg_lam = jnp.maximum(-lam, 0.0) + jnp.log1p(jnp.exp(-jnp.abs(lam)))
    k_neg = LRU_C * softplus_neg_lam
    a = jnp.exp2(r * (k_neg * NEG_LOG2_E))
    m2 = jnp.tanh(r * k_neg) * (1.0 + a * a)
    mult = jnp.where(m2 > 0.0, m2 * lax.rsqrt(m2), 0.0)
    abuf[...] = a
    bbuf[...] = mult * (i_gate * xr)

    h = hcarry[...]
    for t in range(t_len):
        h = abuf[t * nb:(t + 1) * nb, :] * h + bbuf[t * nb:(t + 1) * nb, :]
        for c, hc in enumerate(_lane_chunks(h)):
            hbuf[c, pl.ds(t, nb, stride=pr), :] = hc
    hcarry[...] = h

    hs = jnp.concatenate(
        [jnp.concatenate([hbuf[c, b * pr:b * pr + t_len, :] for c in range(ncl)], axis=1)
         for b in range(nb)], axis=0)
    yr = hs * gbuf[...]

    y = jnp.dot(ua.astype(jnp.bfloat16), wout_ref[0:wc, :], preferred_element_type=f32)
    y = y + jnp.dot(yr.astype(jnp.bfloat16), wout_ref[wc:, :], preferred_element_type=f32)
    o_ref[...] = (x + y).reshape(nb, t_len, d)


def _cast_rows(n_rows, n_steps):
    for rows_blk in range(BF16_TILE_ROWS, n_rows + 1, BF16_TILE_ROWS):
        if n_rows % rows_blk == 0 and n_rows // rows_blk <= n_steps:
            return rows_blk
    raise ValueError((n_rows, n_steps))


def _mixer(x, mix_norm, w_in, conv_dw, conv_dw_bias, conv_ln_g, conv_ln_b,
           lru_conv_w, lru_conv_b, w_gates, b_gates, lru_lambda, w_out, to_cast):
    bsz, s, d = x.shape
    t_len = MIX_T
    rows = bsz * t_len
    wc = conv_dw.shape[1]
    wl = lru_conv_w.shape[1]
    assert bsz == SUBLANES and wc % LANES == 0 and wl % LANES == 0
    assert s % t_len == 0 and t_len % (2 * SUBLANES) == 0
    assert ((LRU_HALO + t_len) // SUBLANES) % 2 == 1
    f32 = jnp.float32
    consts = (mix_norm, w_in, conv_dw, conv_dw_bias, conv_ln_g, conv_ln_b,
              lru_conv_w, lru_conv_b, w_gates, b_gates, lru_lambda, w_out)
    n_steps = s // t_len
    cast_specs, cast_blocks = [], []
    for w in to_cast:
        rows_blk = _cast_rows(w.shape[0], n_steps)
        n_blk = w.shape[0] // rows_blk
        cast_blocks.append(n_blk)
        cast_specs.append(pl.BlockSpec(
            (rows_blk, w.shape[1]), lambda j, n_blk=n_blk: (jnp.minimum(j, n_blk - 1), 0)))
    return pl.pallas_call(
        functools.partial(_mixer_kernel, cast_blocks=tuple(cast_blocks)),
        grid=(n_steps,),
        in_specs=[pl.BlockSpec((bsz, t_len, d), lambda j: (0, j, 0))]
        + [_const_spec(c.shape) for c in consts] + cast_specs,
        out_specs=[pl.BlockSpec((bsz, t_len, d), lambda j: (0, j, 0))] + cast_specs,
        out_shape=[jax.ShapeDtypeStruct((bsz, s, d), f32)]
        + [jax.ShapeDtypeStruct(w.shape, jnp.bfloat16) for w in to_cast],
        scratch_shapes=[
            pltpu.VMEM((rows, wl), f32),
            pltpu.VMEM((wc // LANES, bsz * (CONV_HALO + t_len), LANES), f32),
            pltpu.VMEM((wc // LANES, rows, LANES), f32),
            pltpu.VMEM((wl // LANES, bsz * (LRU_HALO + t_len), LANES), f32),
            pltpu.VMEM((rows, wl), f32),
            pltpu.VMEM((rows, wl), f32),
            pltpu.VMEM((rows, wl), f32),
            pltpu.VMEM((wl // LANES, bsz * (LRU_HALO + t_len), LANES), f32),
            pltpu.VMEM((bsz, wl), f32),
        ],
        compiler_params=pltpu.CompilerParams(
            dimension_semantics=("arbitrary",), vmem_limit_bytes=VMEM_LIMIT),
        name="hybrid_mixer",
    )(x, *consts, *to_cast)


def kernel(x, ffn1_norm, ffn1_w_gate, ffn1_w_up, ffn1_w_down, mix_norm, w_in,
           conv_dw, conv_dw_bias, conv_ln_g, conv_ln_b, lru_conv_w, lru_conv_b,
           lru_w_a, lru_b_a, lru_w_x, lru_b_x, lru_lambda, w_out,
           ffn2_norm, ffn2_w_gate, ffn2_w_up, ffn2_w_down, final_norm):
    bsz, s, d = x.shape
    bf16 = jnp.bfloat16
    row = lambda v: v.reshape(1, -1)

    n_heads, head_dim, _ = lru_w_a.shape
    per_blk = MXU_DEPTH // head_dim
    w_gates = jnp.stack([
        jnp.concatenate([block_diag(*lru_w_a[q:q + per_blk]),
                         block_diag(*lru_w_x[q:q + per_blk])], axis=1)
        for q in range(0, n_heads, per_blk)]).astype(bf16)
    b_gates = jnp.concatenate([lru_b_a, lru_b_x]).reshape(1, -1)

    x1 = _ffn(x.reshape(bsz * s, d), row(ffn1_norm), ffn1_w_gate.astype(bf16),
              ffn1_w_up.astype(bf16), ffn1_w_down.astype(bf16), row(final_norm),
              final_norm=False, name="ffn1")
    x2, w_gate2, w_up2, w_down2 = _mixer(
        x1.reshape(bsz, s, d), row(mix_norm), w_in.astype(bf16),
        conv_dw, row(conv_dw_bias), row(conv_ln_g), row(conv_ln_b),
        lru_conv_w, row(lru_conv_b), w_gates, b_gates, row(lru_lambda),
        w_out.astype(bf16), (ffn2_w_gate, ffn2_w_up, ffn2_w_down))
    y = _ffn(x2.reshape(bsz * s, d), row(ffn2_norm), w_gate2, w_up2, w_down2,
             row(final_norm), final_norm=True, name="ffn2_final")
    return y.reshape(bsz, s, d)
```

```python
import functools

import jax
import jax.numpy as jnp
from jax import lax
from jax.experimental import pallas as pl
from jax.experimental.pallas import tpu as pltpu
from jax.scipy.linalg import block_diag

FFN_RES_SCALE = 0.5
RMS_EPS = 1e-6
LN_EPS = 1e-5
LRU_C = 8.0

SUBLANES = 8
LANES = 128
BF16_TILE_ROWS = 16
CONV_HALO = 32
LRU_HALO = 8

MXU_DEPTH = 256

FFN_TILE = 1024
FFN_CHUNKS = 2
MIX_T = 128
VMEM_LIMIT = 58 * 1024 * 1024

NEG_LOG2_E = -1.4426950408889634
GELU_C0 = 0.7978845608028654
GELU_C1 = 0.7978845608028654 * 0.044715


def _rmsnorm(x, g):
    ms = jnp.mean(x * x, axis=-1, keepdims=True)
    return x * lax.rsqrt(ms + RMS_EPS) * g


def _row_rstd(x):
    return lax.rsqrt(jnp.mean(x * x, axis=-1, keepdims=True) + RMS_EPS)


def _sigmoid(x):
    return 1.0 / (1.0 + jnp.exp2(x * NEG_LOG2_E))


def _ffn_chunks(d_ff):
    tiles = -(-d_ff // MXU_DEPTH)
    per = -(-tiles // FFN_CHUNKS)
    edges = [min(q * per * MXU_DEPTH, d_ff) for q in range(FFN_CHUNKS + 1)]
    return [(lo, hi) for lo, hi in zip(edges[:-1], edges[1:]) if hi > lo]


def _const_spec(shape):
    nd = len(shape)
    return pl.BlockSpec(shape, lambda *_: (0,) * nd, pipeline_mode=pl.Buffered(1))


def _ffn_kernel(x_ref, g_ref, wg_ref, wu_ref, wd_ref, fg_ref, o_ref, *, final_norm):
    x = x_ref[...]
    xg = (x * g_ref[...]).astype(jnp.bfloat16)
    rstd = _row_rstd(x)
    d_ff = wg_ref.shape[1]
    y = None
    for lo, hi in _ffn_chunks(d_ff):
        gate = rstd * jnp.dot(xg, wg_ref[:, lo:hi], preferred_element_type=jnp.float32)
        up = rstd * jnp.dot(xg, wu_ref[:, lo:hi], preferred_element_type=jnp.float32)
        act = (gate * _sigmoid(gate) * up).astype(jnp.bfloat16)
        yq = jnp.dot(act, wd_ref[lo:hi, :], preferred_element_type=jnp.float32)
        y = yq if y is None else y + yq
    out = x + FFN_RES_SCALE * y
    if final_norm:
        out = _rmsnorm(out, fg_ref[...])
    o_ref[...] = out


def _ffn(x2d, norm_g, w_gate, w_up, w_down, final_g, *, final_norm, name):
    n, d = x2d.shape
    d_ff = w_gate.shape[1]
    tm = FFN_TILE
    return pl.pallas_call(
        functools.partial(_ffn_kernel, final_norm=final_norm),
        grid=(n // tm,),
        in_specs=[
            pl.BlockSpec((tm, d), lambda i: (i, 0)),
            _const_spec((1, d)),
            _const_spec((d, d_ff)),
            _const_spec((d, d_ff)),
            _const_spec((d_ff, d)),
            _const_spec((1, d)),
        ],
        out_specs=pl.BlockSpec((tm, d), lambda i: (i, 0)),
        out_shape=jax.ShapeDtypeStruct((n, d), jnp.float32),
        compiler_params=pltpu.CompilerParams(
            dimension_semantics=("arbitrary",), vmem_limit_bytes=VMEM_LIMIT),
        name=name,
    )(x2d, norm_g, w_gate, w_up, w_down, final_g)


def _lane_chunks(v):
    return [v[:, c * LANES:(c + 1) * LANES] for c in range(v.shape[1] // LANES)]


def _mixer_kernel(x_ref, g_ref, win_ref, cw_ref, cb_ref, lng_ref, lnb_ref,
                  rw_ref, rb_ref, wgate_ref, bgate_ref, lam_ref, wout_ref,
                  cast_a_ref, cast_b_ref, cast_c_ref,
                  o_ref, cast_a_out, cast_b_out, cast_c_out,
                  gbuf, ubuf, cbuf, rbuf, xrbuf, abuf, bbuf, hbuf, hcarry, *, cast_blocks):
    nb, t_len, d = x_ref.shape
    rows = nb * t_len
    wc = cw_ref.shape[1]
    wl = rw_ref.shape[1]
    kc = cw_ref.shape[0]
    kr = rw_ref.shape[0]
    ncc = wc // LANES
    ncl = wl // LANES
    pu = CONV_HALO + t_len
    pr = LRU_HALO + t_len
    half = t_len // 2
    f32 = jnp.float32

    for src, dst, n_blk in zip((cast_a_ref, cast_b_ref, cast_c_ref),
                               (cast_a_out, cast_b_out, cast_c_out), cast_blocks):
        if n_blk == pl.num_programs(0):
            dst[...] = src[...].astype(jnp.bfloat16)
        else:
            @pl.when(pl.program_id(0) < n_blk)
            def _(src=src, dst=dst):
                dst[...] = src[...].astype(jnp.bfloat16)

    @pl.when(pl.program_id(0) == 0)
    def _():
        for b in range(nb):
            ubuf[:, b * pu:b * pu + CONV_HALO, :] = jnp.zeros((ncc, CONV_HALO, LANES), f32)
            rbuf[:, b * pr:b * pr + LRU_HALO, :] = jnp.zeros((ncl, LRU_HALO, LANES), f32)
        hcarry[...] = jnp.zeros_like(hcarry)

    x = x_ref[...].reshape(rows, d)
    hn = _rmsnorm(x, g_ref[...]).astype(jnp.bfloat16)

    def project(col0, width):
        return jnp.dot(hn, win_ref[:, col0:col0 + width], preferred_element_type=f32)

    r_gate = project(2 * wc + wl, wl)
    half_g = 0.5 * r_gate
    tanh_g = jnp.tanh(r_gate * (GELU_C0 + GELU_C1 * (r_gate * r_gate)))
    gbuf[...] = half_g + half_g * tanh_g

    c_gate = project(wc, wc)
    c_val = project(0, wc)
    u = c_val * _sigmoid(c_gate)
    for c, uc in enumerate(_lane_chunks(u)):
        for b in range(nb):
            ubuf[c, b * pu + CONV_HALO:b * pu + pu, :] = uc[b * t_len:(b + 1) * t_len, :]
    r_x = project(2 * wc, wl)
    for c, rc in enumerate(_lane_chunks(r_x)):
        for b in range(nb):
            rbuf[c, b * pr + LRU_HALO:b * pr + pr, :] = rc[b * t_len:(b + 1) * t_len, :]

    def conv_seq(b, carry):
        ubase = pl.multiple_of(b * pu, SUBLANES)
        obase = pl.multiple_of(b * t_len, SUBLANES)
        for c in range(ncc):
            lanes = slice(c * LANES, (c + 1) * LANES)
            acc = [jnp.broadcast_to(cb_ref[:, lanes], (half, LANES)) for _ in range(2)]
            for m in range(kc + 1):
                um = ubuf[c, pl.ds(ubase + CONV_HALO - (kc - 1) + m, half, stride=2), :]
                if m < kc:
                    acc[0] = acc[0] + cw_ref[m:m + 1, lanes] * um
                if m >= 1:
                    acc[1] = acc[1] + cw_ref[m - 1:m, lanes] * um
            for p in range(2):
                cbuf[c, pl.ds(obase + p, half, stride=2), :] = acc[p]
        return carry

    lax.fori_loop(0, nb, conv_seq, 0)
    for b in range(nb):
        ubuf[:, b * pu:b * pu + CONV_HALO, :] = ubuf[:, b * pu + t_len:b * pu + pu, :]

    cv = jnp.concatenate([cbuf[c] for c in range(ncc)], axis=1)
    mu = jnp.mean(cv, axis=-1, keepdims=True)
    cc = cv - mu
    var = jnp.mean(cc * cc, axis=-1, keepdims=True)
    ln = cc * lax.rsqrt(var + LN_EPS) * lng_ref[...] + lnb_ref[...]
    ua = ln * _sigmoid(ln)

    for c in range(ncl):
        lanes = slice(c * LANES, (c + 1) * LANES)
        wk = [rw_ref[k:k + 1, lanes] for k in range(kr)]
        bias = rb_ref[:, lanes]
        window = [rbuf[c, pl.ds(LRU_HALO - (kr - 1) + i, nb, stride=pr), :]
                  for i in range(kr - 1)]
        for t in range(t_len):
            window.append(rbuf[c, pl.ds(LRU_HALO + t, nb, stride=pr), :])
            xr_t = bias + wk[0] * window[0]
            for k in range(1, kr):
                xr_t = xr_t + wk[k] * window[k]
            xrbuf[t * nb:(t + 1) * nb, lanes] = xr_t
            window = window[1:]
    for b in range(nb):
        rbuf[:, b * pr:b * pr + LRU_HALO, :] = rbuf[:, b * pr + t_len:b * pr + pr, :]

    xr = xrbuf[...]
    n_blk = wgate_ref.shape[0]
    wb = wl // n_blk
    g_blk = [jnp.dot(xr[:, q * wb:(q + 1) * wb].astype(jnp.bfloat16), wgate_ref[q],
                     preferred_element_type=f32) for q in range(n_blk)]
    bg = bgate_ref[...]
    r = _sigmoid(jnp.concatenate([g[:, 0:wb] for g in g_blk], axis=1) + bg[:, 0:wl])
    i_gate = _sigmoid(jnp.concatenate([g[:, wb:] for g in g_blk], axis=1) + bg[:, wl:])
    lam = lam_ref[...]
    softplus_neg_lam = jnp.maximum(-lam, 0.0) + jnp.log1p(jnp.exp(-jnp.abs(lam)))
    k_neg = LRU_C * softplus_neg_lam
    a = jnp.exp2(r * (k_neg * NEG_LOG2_E))
    m2 = jnp.tanh(r * k_neg) * (1.0 + a * a)
    mult = jnp.where(m2 > 0.0, m2 * lax.rsqrt(m2), 0.0)
    abuf[...] = a
    bbuf[...] = mult * (i_gate * xr)

    h = hcarry[...]
    for t in range(t_len):
        h = abuf[t * nb:(t + 1) * nb, :] * h + bbuf[t * nb:(t + 1) * nb, :]
        for c, hc in enumerate(_lane_chunks(h)):
            hbuf[c, pl.ds(t, nb, stride=pr), :] = hc
    hcarry[...] = h

    hs = jnp.concatenate(
        [jnp.concatenate([hbuf[c, b * pr:b * pr + t_len, :] for c in range(ncl)], axis=1)
         for b in range(nb)], axis=0)
    yr = hs * gbuf[...]

    y = jnp.dot(ua.astype(jnp.bfloat16), wout_ref[0:wc, :], preferred_element_type=f32)
    y = y + jnp.dot(yr.astype(jnp.bfloat16), wout_ref[wc:, :], preferred_element_type=f32)
    o_ref[...] = (x + y).reshape(nb, t_len, d)


def _cast_rows(n_rows, n_steps):
    for rows_blk in range(BF16_TILE_ROWS, n_rows + 1, BF16_TILE_ROWS):
        if n_rows % rows_blk == 0 and n_rows // rows_blk <= n_steps:
            return rows_blk
    raise ValueError((n_rows, n_steps))


def _mixer(x, mix_norm, w_in, conv_dw, conv_dw_bias, conv_ln_g, conv_ln_b,
           lru_conv_w, lru_conv_b, w_gates, b_gates, lru_lambda, w_out, to_cast):
    bsz, s, d = x.shape
    t_len = MIX_T
    rows = bsz * t_len
    wc = conv_dw.shape[1]
    wl = lru_conv_w.shape[1]
    assert bsz == SUBLANES and wc % LANES == 0 and wl % LANES == 0
    assert s % t_len == 0 and t_len % (2 * SUBLANES) == 0
    assert ((LRU_HALO + t_len) // SUBLANES) % 2 == 1
    f32 = jnp.float32
    consts = (mix_norm, w_in, conv_dw, conv_dw_bias, conv_ln_g, conv_ln_b,
              lru_conv_w, lru_conv_b, w_gates, b_gates, lru_lambda, w_out)
    n_steps = s // t_len
    cast_specs, cast_blocks = [], []
    for w in to_cast:
        rows_blk = _cast_rows(w.shape[0], n_steps)
        n_blk = w.shape[0] // rows_blk
        cast_blocks.append(n_blk)
        cast_specs.append(pl.BlockSpec(
            (rows_blk, w.shape[1]), lambda j, n_blk=n_blk: (jnp.minimum(j, n_blk - 1), 0)))
    return pl.pallas_call(
        functools.partial(_mixer_kernel, cast_blocks=tuple(cast_blocks)),
        grid=(n_steps,),
        in_specs=[pl.BlockSpec((bsz, t_len, d), lambda j: (0, j, 0))]
        + [_const_spec(c.shape) for c in consts] + cast_specs,
        out_specs=[pl.BlockSpec((bsz, t_len, d), lambda j: (0, j, 0))] + cast_specs,
        out_shape=[jax.ShapeDtypeStruct((bsz, s, d), f32)]
        + [jax.ShapeDtypeStruct(w.shape, jnp.bfloat16) for w in to_cast],
        scratch_shapes=[
            pltpu.VMEM((rows, wl), f32),
            pltpu.VMEM((wc // LANES, bsz * (CONV_HALO + t_len), LANES), f32),
            pltpu.VMEM((wc // LANES, rows, LANES), f32),
            pltpu.VMEM((wl // LANES, bsz * (LRU_HALO + t_len), LANES), f32),
            pltpu.VMEM((rows, wl), f32),
            pltpu.VMEM((rows, wl), f32),
            pltpu.VMEM((rows, wl), f32),
            pltpu.VMEM((wl // LANES, bsz * (LRU_HALO + t_len), LANES), f32),
            pltpu.VMEM((bsz, wl), f32),
        ],
        compiler_params=pltpu.CompilerParams(
            dimension_semantics=("arbitrary",), vmem_limit_bytes=VMEM_LIMIT),
        name="hybrid_mixer",
    )(x, *consts, *to_cast)


def kernel(x, ffn1_norm, ffn1_w_gate, ffn1_w_up, ffn1_w_down, mix_norm, w_in,
           conv_dw, conv_dw_bias, conv_ln_g, conv_ln_b, lru_conv_w, lru_conv_b,
           lru_w_a, lru_b_a, lru_w_x, lru_b_x, lru_lambda, w_out,
           ffn2_norm, ffn2_w_gate, ffn2_w_up, ffn2_w_down, final_norm):
    bsz, s, d = x.shape
    bf16 = jnp.bfloat16
    row = lambda v: v.reshape(1, -1)

    n_heads, head_dim, _ = lru_w_a.shape
    per_blk = MXU_DEPTH // head_dim
    w_gates = jnp.stack([
        jnp.concatenate([block_diag(*lru_w_a[q:q + per_blk]),
                         block_diag(*lru_w_x[q:q + per_blk])], axis=1)
        for q in range(0, n_heads, per_blk)]).astype(bf16)
    b_gates = jnp.concatenate([lru_b_a, lru_b_x]).reshape(1, -1)

    x1 = _ffn(x.reshape(bsz * s, d), row(ffn1_norm), ffn1_w_gate.astype(bf16),
              ffn1_w_up.astype(bf16), ffn1_w_down.astype(bf16), row(final_norm),
              final_norm=False, name="ffn1")
    x2, w_gate2, w_up2, w_down2 = _mixer(
        x1.reshape(bsz, s, d), row(mix_norm), w_in.astype(bf16),
        conv_dw, row(conv_dw_bias), row(conv_ln_g), row(conv_ln_b),
        lru_conv_w, row(lru_conv_b), w_gates, b_gates, row(lru_lambda),
        w_out.astype(bf16), (ffn2_w_gate, ffn2_w_up, ffn2_w_down))
    y = _ffn(x2.reshape(bsz * s, d), row(ffn2_norm), w_gate2, w_up2, w_down2,
             row(final_norm), final_norm=True, name="ffn2_final")
    return y.reshape(bsz, s, d)
```
